```python
import jax, jax.numpy as jnp
from jax import lax
import numpy as np

D_MODEL = 1024
BATCH = 8
SEQ = 2048
DEPTH = 4
DEC_BATCH = 32
DEC_SEQ = 32
PAST_LEN = 1024

CHUNK = 64
WINDOW = 128
WIN_CHUNKS = WINDOW // CHUNK
HEAD_DIM = 64
N_HEADS = D_MODEL // HEAD_DIM
N_KV_HEADS = 4
GROUP = N_HEADS // N_KV_HEADS
Q_DIM = N_HEADS * HEAD_DIM
KV_DIM = N_KV_HEADS * HEAD_DIM
CONV_DIM = D_MODEL
CONV_WIDTH = 3
N_MEM = 256
MEM_HEADS = 4
MEM_HEAD_DIM = 64
MEM_DIM = MEM_HEADS * MEM_HEAD_DIM
D_FF = 4 * D_MODEL
EPS = 1e-6
ATTN_SCALE = HEAD_DIM ** -0.5
MEM_SCALE = MEM_HEAD_DIM ** -0.5
IN_COLS = Q_DIM + 2 * KV_DIM + 3 * CONV_DIM + 2 * D_MODEL
SPLITS = (Q_DIM, Q_DIM + KV_DIM, Q_DIM + 2 * KV_DIM, Q_DIM + 2 * KV_DIM + CONV_DIM,
          Q_DIM + 2 * KV_DIM + 2 * CONV_DIM, Q_DIM + 2 * KV_DIM + 3 * CONV_DIM,
          Q_DIM + 2 * KV_DIM + 3 * CONV_DIM + D_MODEL)

kernel_name = 'hybrid_chunk_stream_swa_sink_shortconv_step'


def rms_norm(x, g):
    xf = x.astype(jnp.float32)
    y = xf * lax.rsqrt(jnp.mean(xf * xf, axis=-1, keepdims=True) + EPS)
    return (y * g.astype(jnp.float32)).astype(x.dtype)


def alibi_slopes():
    return jnp.exp2(-8.0 * jnp.arange(1, N_HEADS + 1, dtype=jnp.float32) / N_HEADS)


def sink_softmax(s, sink):
    m = jnp.maximum(jnp.max(s, axis=-1, keepdims=True), sink)
    p = jnp.exp(s - m)
    return p / (jnp.sum(p, axis=-1, keepdims=True) + jnp.exp(sink - m))


def window_attn_prompt(q, k, v, sink):
    B, S = q.shape[0], q.shape[1]
    nb = S // CHUNK
    J = (WIN_CHUNKS + 1) * CHUNK
    pad = ((0, 0), (WIN_CHUNKS * CHUNK, 0), (0, 0), (0, 0))
    kp = jnp.pad(k, pad).reshape(B, nb + WIN_CHUNKS, CHUNK, N_KV_HEADS, HEAD_DIM)
    vp = jnp.pad(v, pad).reshape(B, nb + WIN_CHUNKS, CHUNK, N_KV_HEADS, HEAD_DIM)
    kb = jnp.concatenate([kp[:, j:j + nb] for j in range(WIN_CHUNKS + 1)], axis=2)
    vb = jnp.concatenate([vp[:, j:j + nb] for j in range(WIN_CHUNKS + 1)], axis=2)
    qb = q.reshape(B, nb, CHUNK, N_KV_HEADS, GROUP, HEAD_DIM)
    s = jnp.einsum('bnqkgd,bnjkd->bnkgqj', qb, kb, preferred_element_type=jnp.float32) * ATTN_SCALE
    qpos = jnp.arange(S, dtype=jnp.int32).reshape(nb, CHUNK)
    kpos = (jnp.arange(nb, dtype=jnp.int32)[:, None] - WIN_CHUNKS) * CHUNK + jnp.arange(J, dtype=jnp.int32)[None, :]
    dist = jnp.abs(qpos[:, :, None] - kpos[:, None, :]).astype(jnp.float32)
    slopes = alibi_slopes().reshape(N_KV_HEADS, GROUP)
    bias = -slopes[None, None, :, :, None, None] * dist[None, :, None, None, :, :]
    valid = (kpos >= 0)[None, :, None, None, None, :]
    s = jnp.where(valid, s + bias, jnp.finfo(jnp.float32).min)
    sk = sink.astype(jnp.float32).reshape(N_KV_HEADS, GROUP)[None, None, :, :, None, None]
    p = sink_softmax(s, sk)
    o = jnp.einsum('bnkgqj,bnjkd->bnqkgd', p.astype(v.dtype), vb)
    return o.reshape(B, S, Q_DIM)


def window_attn_sample(q, k_new, v_new, k_cache, v_cache, sink):
    Bd, n = q.shape[0], q.shape[1]
    rows = k_cache.shape[1]
    kk = jnp.concatenate([k_cache, k_new], axis=1)
    vv = jnp.concatenate([v_cache, v_new], axis=1)
    qg = q.reshape(Bd, n, N_KV_HEADS, GROUP, HEAD_DIM)
    s = jnp.einsum('bqkgd,bjkd->bkgqj', qg, kk, preferred_element_type=jnp.float32) * ATTN_SCALE
    qpos = PAST_LEN + jnp.arange(n, dtype=jnp.int32)
    kpos = PAST_LEN - rows + jnp.arange(rows + n, dtype=jnp.int32)
    dist = jnp.abs(qpos[:, None] - kpos[None, :]).astype(jnp.float32)
    slopes = alibi_slopes().reshape(N_KV_HEADS, GROUP)
    s = s - slopes[None, :, :, None, None] * dist[None, None, None]
    sk = sink.astype(jnp.float32).reshape(N_KV_HEADS, GROUP)[None, :, :, None, None]
    p = sink_softmax(s, sk)
    o = jnp.einsum('bkgqj,bjkd->bqkgd', p.astype(vv.dtype), vv)
    return o.reshape(Bd, n, Q_DIM)


def memory_kv(mem, g_mem, w_ckv):
    kv = rms_norm(mem, g_mem) @ w_ckv
    mk, mv = jnp.split(kv, 2, axis=-1)
    B = mem.shape[0]
    return (mk.reshape(B, N_MEM, MEM_HEADS, MEM_HEAD_DIM), mv.reshape(B, N_MEM, MEM_HEADS, MEM_HEAD_DIM))


def trunk_layer(x, attn_core, conv_left, mem_k, mem_v, g_mix, w_in, conv_w, w_attn_out, w_conv_out,
                w_mix_out, g_cross, w_cq, w_co, g_mlp, w_up, w_down):
    B, n = x.shape[0], x.shape[1]
    h = rms_norm(x, g_mix)
    z = h @ w_in
    q, k, v, ch, cb, cc, ga, gb = jnp.split(z, SPLITS, axis=-1)
    q = q.reshape(B, n, N_HEADS, HEAD_DIM)
    k = k.reshape(B, n, N_KV_HEADS, HEAD_DIM)
    v = v.reshape(B, n, N_KV_HEADS, HEAD_DIM)
    a = attn_core(q, k, v) @ w_attn_out
    u = cc * ch
    up = jnp.concatenate([conv_left, u], axis=1)
    conv = up[:, 0:n] * conv_w[0]
    for j in range(1, CONV_WIDTH):
        conv = conv + up[:, j:j + n] * conv_w[j]
    bo = (cb * conv) @ w_conv_out
    x = x + (jax.nn.sigmoid(ga) * a + jax.nn.sigmoid(gb) * bo) @ w_mix_out
    hc = rms_norm(x, g_cross)
    cq = (hc @ w_cq).reshape(B, n, MEM_HEADS, MEM_HEAD_DIM)
    s = jnp.einsum('bqhd,bmhd->bhqm', cq, mem_k, preferred_element_type=jnp.float32) * MEM_SCALE
    p = jax.nn.softmax(s, axis=-1)
    co = jnp.einsum('bhqm,bmhd->bqhd', p.astype(mem_v.dtype), mem_v).reshape(B, n, MEM_DIM)
    x = x + co @ w_co
    hm = rms_norm(x, g_mlp)
    x = x + jnp.square(jax.nn.relu(hm @ w_up)) @ w_down
    return x, k, v, up[:, -(CONV_WIDTH - 1):]


def setup_inputs(seed: int = 0) -> dict:
    key = jax.random.key(seed)
    ks = jax.random.split(key, 24)
    f32 = jnp.float32
    win_rows = min(WINDOW, PAST_LEN)

    def nrm(k, shape, scale):
        return jax.random.normal(k, shape, f32) * scale

    def gain(k, shape):
        return 1.0 + 0.05 * jax.random.normal(k, shape, f32)

    return {
        'x_prompt': nrm(ks[0], (BATCH, SEQ, D_MODEL), 1.0),
        'x_sample': nrm(ks[1], (DEC_BATCH, DEC_SEQ, D_MODEL), 1.0),
        'mem_prompt': nrm(ks[2], (BATCH, N_MEM, D_MODEL), 1.0),
        'cache_attn_k': nrm(ks[3], (DEPTH, DEC_BATCH, win_rows, N_KV_HEADS, HEAD_DIM), 1.0),
        'cache_attn_v': nrm(ks[4], (DEPTH, DEC_BATCH, win_rows, N_KV_HEADS, HEAD_DIM), 1.0),
        'state_conv': nrm(ks[5], (DEPTH, DEC_BATCH, CONV_WIDTH - 1, CONV_DIM), 1.0),
        'cache_mem_k': nrm(ks[6], (DEPTH, DEC_BATCH, N_MEM, MEM_HEADS, MEM_HEAD_DIM), 1.0),
        'cache_mem_v': nrm(ks[7], (DEPTH, DEC_BATCH, N_MEM, MEM_HEADS, MEM_HEAD_DIM), 1.0),
        'norm_mix_g': gain(ks[8], (DEPTH, D_MODEL)),
        'w_in': nrm(ks[9], (DEPTH, D_MODEL, IN_COLS), D_MODEL ** -0.5),
        'conv_w': nrm(ks[10], (DEPTH, CONV_WIDTH, CONV_DIM), CONV_WIDTH ** -0.5),
        'attn_sink': nrm(ks[11], (DEPTH, N_HEADS), 0.5),
        'w_attn_out': nrm(ks[12], (DEPTH, Q_DIM, D_MODEL), Q_DIM ** -0.5),
        'w_conv_out': nrm(ks[13], (DEPTH, CONV_DIM, D_MODEL), CONV_DIM ** -0.5),
        'w_mix_out': nrm(ks[14], (DEPTH, D_MODEL, D_MODEL), D_MODEL ** -0.5),
        'norm_cross_g': gain(ks[15], (DEPTH, D_MODEL)),
        'norm_mem_g': gain(ks[16], (DEPTH, D_MODEL)),
        'w_cq': nrm(ks[17], (DEPTH, D_MODEL, MEM_DIM), D_MODEL ** -0.5),
        'w_ckv': nrm(ks[18], (DEPTH, D_MODEL, 2 * MEM_DIM), D_MODEL ** -0.5),
        'w_co': nrm(ks[19], (DEPTH, MEM_DIM, D_MODEL), MEM_DIM ** -0.5),
        'norm_mlp_g': gain(ks[20], (DEPTH, D_MODEL)),
        'w_up': nrm(ks[21], (DEPTH, D_MODEL, D_FF), D_MODEL ** -0.5),
        'w_down': nrm(ks[22], (DEPTH, D_FF, D_MODEL), 0.5 * D_FF ** -0.5),
        'norm_final_g': gain(ks[23], (D_MODEL,)),
    }


def reference(x_prompt, x_sample, mem_prompt, cache_attn_k, cache_attn_v, state_conv, cache_mem_k, cache_mem_v,
              norm_mix_g, w_in, conv_w, attn_sink, w_attn_out, w_conv_out, w_mix_out, norm_cross_g, norm_mem_g,
              w_cq, w_ckv, w_co, norm_mlp_g, w_up, w_down, norm_final_g):
    xp = x_prompt
    xs = x_sample
    prompt_rows = min(WINDOW, xp.shape[1])
    kp_l, vp_l, cp_l, mkp_l, mvp_l = [], [], [], [], []
    ks_l, vs_l, cs_l = [], [], []
    for l in range(DEPTH):
        sink = attn_sink[l]
        shared = (norm_mix_g[l], w_in[l], conv_w[l], w_attn_out[l], w_conv_out[l], w_mix_out[l],
                  norm_cross_g[l], w_cq[l], w_co[l], norm_mlp_g[l], w_up[l], w_down[l])
        mk, mv = memory_kv(mem_prompt, norm_mem_g[l], w_ckv[l])
        zero_left = jnp.zeros((xp.shape[0], CONV_WIDTH - 1, CONV_DIM), xp.dtype)
        xp, kp, vp, cpst = trunk_layer(xp, lambda q, k, v: window_attn_prompt(q, k, v, sink),
                                       zero_left, mk, mv, *shared)
        kp_l.append(kp[:, -prompt_rows:])
        vp_l.append(vp[:, -prompt_rows:])
        cp_l.append(cpst)
        mkp_l.append(mk)
        mvp_l.append(mv)
        kc, vc = cache_attn_k[l], cache_attn_v[l]
        xs, kn, vn, csst = trunk_layer(xs, lambda q, k, v: window_attn_sample(q, k, v, kc, vc, sink),
                                       state_conv[l], cache_mem_k[l], cache_mem_v[l], *shared)
        ks_l.append(kn)
        vs_l.append(vn)
        cs_l.append(csst)
    y_prompt = rms_norm(xp, norm_final_g)
    y_sample = rms_norm(xs, norm_final_g)
    return (y_prompt, y_sample,
            jnp.stack(kp_l), jnp.stack(vp_l), jnp.stack(cp_l), jnp.stack(mkp_l), jnp.stack(mvp_l),
            jnp.stack(ks_l), jnp.stack(vs_l), jnp.stack(cs_l))
```

```python
import functools

import jax
import jax.numpy as jnp
from jax import lax
from jax.experimental import pallas as pl
from jax.experimental.pallas import tpu as pltpu

D_MODEL = 1024
DEPTH = 4
CHUNK = 64
WINDOW = 128
WIN_CHUNKS = WINDOW // CHUNK
HEAD_DIM = 64
N_HEADS = 16
N_KV_HEADS = 4
GROUP = N_HEADS // N_KV_HEADS
KV_DIM = N_KV_HEADS * HEAD_DIM
N_MEM = 256
MEM_HEADS = 4
MEM_DIM = 256
D_FF = 4 * D_MODEL
EPS = 1e-6
ATTN_SCALE = HEAD_DIM ** -0.5
MEM_SCALE = (MEM_DIM // MEM_HEADS) ** -0.5
CONV_TAPS = 3
CONV_LEFT = CONV_TAPS - 1

LANES = 128
SUBLANES = 8
HALF = LANES // 2
SCORE_COLS = 2 * LANES
MASKED = -1e30
V7X_VMEM_BYTES = 64 * 1024 * 1024
VMEM_LIMIT = V7X_VMEM_BYTES - 8 * 1024 * 1024

PROMPT_TILE = 256
SAMPLE_SEQS = 8
COLS = 256

BF16 = jnp.bfloat16
F32 = jnp.float32


def _dot(a, b):
    return jnp.dot(a, b, preferred_element_type=F32)


def _dot_nt(a, b):
    return lax.dot_general(a, b, (((1,), (1,)), ((), ())), preferred_element_type=F32)


def _rms(x, g):
    return x * lax.rsqrt(jnp.mean(x * x, axis=-1, keepdims=True) + EPS) * g


def _low_half(shape):
    return lax.broadcasted_iota(jnp.int32, shape, len(shape) - 1) < HALF


def _dup_halves(col):
    swapped = pltpu.roll(col, HALF, axis=1)
    lo = _low_half(col.shape)
    return jnp.where(lo, col, swapped), jnp.where(lo, swapped, col)


def _store_lane_tiled(dst_ref, kv, store):
    for c in range(KV_DIM // LANES):
        even, odd = _dup_halves(kv[:, c * LANES:(c + 1) * LANES])
        store(dst_ref, 2 * c, even)
        store(dst_ref, 2 * c + 1, odd)


def _pad_keys(block):
    pad = jnp.zeros((SCORE_COLS - block.shape[0], block.shape[1]), block.dtype)
    return jnp.concatenate([block, pad], axis=0)


def _head_rows(qa, qb):
    lo = _low_half(qa.shape)
    zero = jnp.zeros_like(qa)
    blocks = (
        jnp.concatenate([jnp.where(lo, qa, zero), zero], axis=1),
        jnp.concatenate([jnp.where(lo, zero, qa), zero], axis=1),
        jnp.concatenate([zero, jnp.where(lo, qb, zero)], axis=1),
        jnp.concatenate([zero, jnp.where(lo, zero, qb)], axis=1),
    )
    return jnp.concatenate(blocks, axis=0)


def _pick_heads(o4, rows):
    lo = _low_half((rows, LANES))
    oa = jnp.where(lo, o4[0:rows, 0:LANES], o4[rows:2 * rows, 0:LANES])
    ob = jnp.where(lo, o4[2 * rows:3 * rows, LANES:], o4[3 * rows:4 * rows, LANES:])
    return oa, ob


def _build_score_bias(bias_ref, slopes_ref, sink_ref, layer, nq, nk, first_valid_keys):
    qi = lax.broadcasted_iota(jnp.int32, (nq, SCORE_COLS), 0)
    ji = lax.broadcasted_iota(jnp.int32, (nq, SCORE_COLS), 1)
    dist = jnp.abs(qi + WINDOW - ji).astype(F32)
    for h in range(N_HEADS):
        base = jnp.where(ji == nk, sink_ref[layer, h], -slopes_ref[h] * dist)
        for v, first in enumerate(first_valid_keys):
            valid = (ji >= first) & (ji <= nk)
            bias_ref[v, h * nq:(h + 1) * nq, :] = jnp.where(valid, base, MASKED)


def _softmax(s):
    p = jnp.exp(s - jnp.max(s, axis=-1, keepdims=True))
    return p * (1.0 / jnp.sum(p, axis=-1, keepdims=True))


def _attend_group(q_s, a_s, r0, nq, kv_head, kblock, vblock, bias):
    c0 = kv_head * GROUP * HEAD_DIM
    qs = _head_rows(q_s[pl.ds(r0, nq), c0:c0 + LANES], q_s[pl.ds(r0, nq), c0 + LANES:c0 + 2 * LANES])
    p = _softmax(_dot_nt(qs, _pad_keys(kblock)) + bias)
    oa, ob = _pick_heads(_dot(p.astype(BF16), _pad_keys(vblock)), nq)
    a_s[pl.ds(r0, nq), c0:c0 + LANES] = oa.astype(BF16)
    a_s[pl.ds(r0, nq), c0 + LANES:c0 + 2 * LANES] = ob.astype(BF16)


def _project_qkv(h_s, q_s, wqkv_ref):
    hb = h_s[...]
    for c in range(0, D_MODEL, COLS):
        q_s[:, c:c + COLS] = (_dot(hb, wqkv_ref[:, c:c + COLS]) * ATTN_SCALE).astype(BF16)
    k = _dot(hb, wqkv_ref[:, D_MODEL:D_MODEL + KV_DIM])
    v = _dot(hb, wqkv_ref[:, D_MODEL + KV_DIM:D_MODEL + 2 * KV_DIM])
    return k, v


def _conv_and_merge(h_s, a_s, boin_s, mix_s, u3, rows, x_ref, xo_ref, co_ref,
                    wconv_ref, wgate_ref, convw_ref, wao_ref, wco_ref, wmo_ref):
    groups = u3.shape[0]
    total = groups * rows
    for c in range(0, D_MODEL, COLS):
        cs = slice(c, c + COLS)
        hb = h_s[...]
        ch = _dot(hb, wconv_ref[:, c:c + COLS])
        cc = _dot(hb, wconv_ref[:, 2 * D_MODEL + c:2 * D_MODEL + c + COLS])
        u = cc * ch
        u3[:, SUBLANES:SUBLANES + rows, cs] = u.reshape(groups, rows, COLS)
        um2 = u3[:, SUBLANES - 2:SUBLANES - 2 + rows, cs].reshape(total, COLS)
        um1 = u3[:, SUBLANES - 1:SUBLANES - 1 + rows, cs].reshape(total, COLS)
        conv = um2 * convw_ref[0:1, cs] + um1 * convw_ref[1:2, cs] + u * convw_ref[2:3, cs]
        cb = _dot(hb, wconv_ref[:, D_MODEL + c:D_MODEL + c + COLS])
        boin_s[:, cs] = (cb * conv).astype(BF16)
        tail = u3[:, SUBLANES + rows - CONV_LEFT:SUBLANES + rows, cs]
        co_ref[:, :, cs] = tail
        u3[:, SUBLANES - CONV_LEFT:SUBLANES, cs] = tail
    for c in range(0, D_MODEL, COLS):
        cs = slice(c, c + COLS)
        hb = h_s[...]
        ga = _dot(hb, wgate_ref[:, c:c + COLS])
        gb = _dot(hb, wgate_ref[:, D_MODEL + c:D_MODEL + c + COLS])
        a = _dot(a_s[...], wao_ref[:, cs])
        bo = _dot(boin_s[...], wco_ref[:, cs])
        mix_s[:, cs] = (jax.nn.sigmoid(ga) * a + jax.nn.sigmoid(gb) * bo).astype(BF16)
    for c in range(0, D_MODEL, COLS):
        cs = slice(c, c + COLS)
        xo_ref[:, cs] = x_ref[:, cs] + _dot(mix_s[...], wmo_ref[:, cs])


def _mixer_prompt_kernel(slopes_ref, sink_ref, x_ref, g_ref, wqkv_ref, wconv_ref, wgate_ref, convw_ref,
                         wao_ref, wco_ref, wmo_ref,
                         xo_ref, ko_ref, vo_ref, co_ref,
                         h_s, q_s, k4_s, v4_s, a_s, u_s, boin_s, mix_s, bias_s, *, layer, tile):
    t = pl.program_id(1)
    keys = WINDOW + CHUNK

    @pl.when(t == 0)
    def _():
        _build_score_bias(bias_s, slopes_ref, sink_ref, layer, CHUNK, keys, (WINDOW, CHUNK, 0))
        k4_s[:, 0:WINDOW, :] = jnp.zeros((N_KV_HEADS, WINDOW, 2 * LANES), BF16)
        v4_s[:, 0:WINDOW, :] = jnp.zeros((N_KV_HEADS, WINDOW, 2 * LANES), BF16)
        u_s[:, 0:SUBLANES, :] = jnp.zeros((1, SUBLANES, D_MODEL), F32)

    h_s[...] = _rms(x_ref[...], g_ref[...]).astype(BF16)
    k, v = _project_qkv(h_s, q_s, wqkv_ref)
    ko_ref[...] = k[tile - WINDOW:, :]
    vo_ref[...] = v[tile - WINDOW:, :]

    def store_new(dst_ref, head, dup):
        dup = dup.astype(BF16)
        dst_ref[head, WINDOW:WINDOW + tile, 0:LANES] = dup
        dst_ref[head, WINDOW:WINDOW + tile, LANES:] = dup

    _store_lane_tiled(k4_s, k, store_new)
    _store_lane_tiled(v4_s, v, store_new)

    def attend(n, carry):
        r0 = pl.multiple_of(n * CHUNK, CHUNK)
        variant = jnp.where(t == 0, jnp.minimum(n, WIN_CHUNKS), WIN_CHUNKS)
        for kh in range(N_KV_HEADS):
            bias = bias_s[variant, kh * GROUP * CHUNK:(kh + 1) * GROUP * CHUNK, :]
            _attend_group(q_s, a_s, r0, CHUNK, kh, k4_s[kh, pl.ds(r0, keys), :], v4_s[kh, pl.ds(r0, keys), :], bias)
        return carry

    lax.fori_loop(0, tile // CHUNK, attend, 0)

    k4_s[:, 0:WINDOW, :] = k4_s[:, tile:tile + WINDOW, :]
    v4_s[:, 0:WINDOW, :] = v4_s[:, tile:tile + WINDOW, :]

    _conv_and_merge(h_s, a_s, boin_s, mix_s, u_s, tile, x_ref, xo_ref, co_ref,
                    wconv_ref, wgate_ref, convw_ref, wao_ref, wco_ref, wmo_ref)


def _mixer_sample_kernel(slopes_ref, sink_ref, x_ref, g_ref, kc_ref, vc_ref, st_ref,
                         wqkv_ref, wconv_ref, wgate_ref, convw_ref, wao_ref, wco_ref, wmo_ref,
                         xo_ref, ko_ref, vo_ref, co_ref,
                         h_s, q_s, k4_s, v4_s, a_s, u_s, boin_s, mix_s, bias_s, *, layer, seqs, rows):
    keys = WINDOW + rows

    @pl.when(pl.program_id(0) == 0)
    def _():
        _build_score_bias(bias_s, slopes_ref, sink_ref, layer, rows, keys, (0,))

    u_s[:, SUBLANES - CONV_LEFT:SUBLANES, :] = st_ref[...]

    h_s[...] = _rms(x_ref[...], g_ref[...]).astype(BF16)
    k, v = _project_qkv(h_s, q_s, wqkv_ref)
    ko_ref[...] = k
    vo_ref[...] = v

    def store_cache(dst_ref, head, dup):
        dup = dup.reshape(seqs, WINDOW, LANES).astype(BF16)
        dst_ref[head, :, 0:WINDOW, 0:LANES] = dup
        dst_ref[head, :, 0:WINDOW, LANES:] = dup

    def store_new(dst_ref, head, dup):
        dup = dup.reshape(seqs, rows, LANES).astype(BF16)
        dst_ref[head, :, WINDOW:keys, 0:LANES] = dup
        dst_ref[head, :, WINDOW:keys, LANES:] = dup

    _store_lane_tiled(k4_s, kc_ref[...].reshape(seqs * WINDOW, KV_DIM), store_cache)
    _store_lane_tiled(v4_s, vc_ref[...].reshape(seqs * WINDOW, KV_DIM), store_cache)
    _store_lane_tiled(k4_s, k, store_new)
    _store_lane_tiled(v4_s, v, store_new)

    def attend(g, carry):
        r0 = pl.multiple_of(g * rows, rows)
        for kh in range(N_KV_HEADS):
            bias = bias_s[0, kh * GROUP * rows:(kh + 1) * GROUP * rows, :]
            _attend_group(q_s, a_s, r0, rows, kh, k4_s[kh, g], v4_s[kh, g], bias)
        return carry

    lax.fori_loop(0, seqs, attend, 0)

    _conv_and_merge(h_s, a_s, boin_s, mix_s, u_s, rows, x_ref, xo_ref, co_ref,
                    wconv_ref, wgate_ref, convw_ref, wao_ref, wco_ref, wmo_ref)


def _ffn_kernel(x_ref, mk_ref, mv_ref, gc_ref, wcq_ref, wcout_ref, gm_ref, wup_ref, wdown_ref, gf_ref,
                xo_ref,
                h_s, cq_s, co_s, x1_s, act_s, *, groups, rows, final):
    h_s[...] = _rms(x_ref[...], gc_ref[...]).astype(BF16)
    cq_s[...] = (_dot(h_s[...], wcq_ref[...]) * MEM_SCALE).astype(BF16)
    lo = _low_half((rows, LANES))
    zero = jnp.zeros((rows, LANES), BF16)
    for g in range(groups):
        rs = slice(g * rows, (g + 1) * rows)
        mk = mk_ref[g].astype(BF16)
        mv = mv_ref[g].astype(BF16)
        outs = []
        for hh in range(MEM_HEADS):
            c, odd = divmod(hh, 2)
            qcol = cq_s[rs, c * LANES:(c + 1) * LANES]
            qm = jnp.where(lo, zero, qcol) if odd else jnp.where(lo, qcol, zero)
            qm = jnp.concatenate([qm, zero] if c == 0 else [zero, qm], axis=1)
            p = _softmax(_dot_nt(qm, mk))
            outs.append(_dot(p.astype(BF16), mv)[:, c * LANES:(c + 1) * LANES])
        co_s[rs, 0:LANES] = jnp.where(lo, outs[0], outs[1]).astype(BF16)
        co_s[rs, LANES:] = jnp.where(lo, outs[2], outs[3]).astype(BF16)
    for c in range(0, D_MODEL, COLS):
        cs = slice(c, c + COLS)
        x1_s[:, cs] = x_ref[:, cs] + _dot(co_s[...], wcout_ref[:, cs])
    h_s[...] = _rms(x1_s[...], gm_ref[...]).astype(BF16)
    for c in range(0, D_FF, COLS):
        up = jnp.maximum(_dot(h_s[...], wup_ref[:, c:c + COLS]), 0.0)
        act_s[:, c:c + COLS] = (up * up).astype(BF16)
    for c in range(0, D_MODEL, COLS):
        cs = slice(c, c + COLS)
        x1_s[:, cs] = x1_s[:, cs] + _dot(act_s[...], wdown_ref[:, cs])
    if final:
        xo_ref[...] = _rms(x1_s[...], gf_ref[...])
    else:
        xo_ref[...] = x1_s[...]


def _memkv_kernel(mem_ref, g_ref, w_ref, mk_ref, mv_ref):
    h = _rms(mem_ref[...], g_ref[...]).astype(BF16)
    kv = _dot(h, w_ref[...])
    mk_ref[...] = kv[:, 0:MEM_DIM]
    mv_ref[...] = kv[:, MEM_DIM:]


def _params(n_axes):
    return pltpu.CompilerParams(dimension_semantics=("arbitrary",) * n_axes, vmem_limit_bytes=VMEM_LIMIT)


def _resident(shape, layer):
    zeros = (0,) * len(shape)
    return pl.BlockSpec((None,) + tuple(shape), lambda *_: (layer,) + zeros, pipeline_mode=pl.Buffered(1))


def _smem():
    return pl.BlockSpec(memory_space=pltpu.SMEM)


def _mixer_weight_specs(layer):
    return [
        _resident((D_MODEL, D_MODEL + 2 * KV_DIM), layer),
        _resident((D_MODEL, 3 * D_MODEL), layer),
        _resident((D_MODEL, 2 * D_MODEL), layer),
        _resident((CONV_TAPS, D_MODEL), layer),
        _resident((D_MODEL, D_MODEL), layer),
        _resident((D_MODEL, D_MODEL), layer),
        _resident((D_MODEL, D_MODEL), layer),
    ]


def _mixer_weights(w):
    return (w["w_qkv"], w["w_conv"], w["w_gate"], w["conv_w"], w["w_attn_out"], w["w_conv_out"], w["w_mix_out"])


def _mixer_prompt(layer, x, slopes, w):
    batch, seq, _ = x.shape
    tile = PROMPT_TILE
    row = lambda b, t: (b, t, 0)
    per_seq = lambda b, t: (b, 0, 0)
    return pl.pallas_call(
        functools.partial(_mixer_prompt_kernel, layer=layer, tile=tile),
        grid=(batch, seq // tile),
        in_specs=[_smem(), _smem(),
                  pl.BlockSpec((None, tile, D_MODEL), row),
                  _resident((1, D_MODEL), layer)] + _mixer_weight_specs(layer),
        out_specs=[pl.BlockSpec((None, tile, D_MODEL), row),
                   pl.BlockSpec((None, WINDOW, KV_DIM), per_seq),
                   pl.BlockSpec((None, WINDOW, KV_DIM), per_seq),
                   pl.BlockSpec((1, CONV_LEFT, D_MODEL), per_seq)],
        out_shape=[jax.ShapeDtypeStruct(x.shape, F32),
                   jax.ShapeDtypeStruct((batch, WINDOW, KV_DIM), F32),
                   jax.ShapeDtypeStruct((batch, WINDOW, KV_DIM), F32),
                   jax.ShapeDtypeStruct((batch, CONV_LEFT, D_MODEL), F32)],
        scratch_shapes=[
            pltpu.VMEM((tile, D_MODEL), BF16),
            pltpu.VMEM((tile, D_MODEL), BF16),
            pltpu.VMEM((N_KV_HEADS, WINDOW + tile, 2 * LANES), BF16),
            pltpu.VMEM((N_KV_HEADS, WINDOW + tile, 2 * LANES), BF16),
            pltpu.VMEM((tile, D_MODEL), BF16),
            pltpu.VMEM((1, SUBLANES + tile, D_MODEL), F32),
            pltpu.VMEM((tile, D_MODEL), BF16),
            pltpu.VMEM((tile, D_MODEL), BF16),
            pltpu.VMEM((WIN_CHUNKS + 1, N_HEADS * CHUNK, SCORE_COLS), F32),
        ],
        compiler_params=_params(2),
        name=f"mixer_prompt_l{layer}",
    )(slopes, w["sink"], x, w["g_mix"], *_mixer_weights(w))


def _mixer_sample(layer, x2, slopes, kc, vc, st, w, rows):
    total = x2.shape[0]
    seqs = SAMPLE_SEQS
    tile = seqs * rows
    keys = WINDOW + rows
    row = lambda i: (i, 0)
    per_seq = lambda i: (layer, i, 0, 0)
    return pl.pallas_call(
        functools.partial(_mixer_sample_kernel, layer=layer, seqs=seqs, rows=rows),
        grid=(total // tile,),
        in_specs=[_smem(), _smem(),
                  pl.BlockSpec((tile, D_MODEL), row),
                  _resident((1, D_MODEL), layer),
                  pl.BlockSpec((None, seqs, WINDOW, KV_DIM), per_seq),
                  pl.BlockSpec((None, seqs, WINDOW, KV_DIM), per_seq),
                  pl.BlockSpec((None, seqs, CONV_LEFT, D_MODEL), per_seq),
                  ] + _mixer_weight_specs(layer),
        out_specs=[pl.BlockSpec((tile, D_MODEL), row),
                   pl.BlockSpec((tile, KV_DIM), row),
                   pl.BlockSpec((tile, KV_DIM), row),
                   pl.BlockSpec((seqs, CONV_LEFT, D_MODEL), lambda i: (i, 0, 0))],
        out_shape=[jax.ShapeDtypeStruct(x2.shape, F32),
                   jax.ShapeDtypeStruct((total, KV_DIM), F32),
                   jax.ShapeDtypeStruct((total, KV_DIM), F32),
                   jax.ShapeDtypeStruct((total // rows, CONV_LEFT, D_MODEL), F32)],
        scratch_shapes=[
            pltpu.VMEM((tile, D_MODEL), BF16),
            pltpu.VMEM((tile, D_MODEL), BF16),
            pltpu.VMEM((N_KV_HEADS, seqs, keys, 2 * LANES), BF16),
            pltpu.VMEM((N_KV_HEADS, seqs, keys, 2 * LANES), BF16),
            pltpu.VMEM((tile, D_MODEL), BF16),
            pltpu.VMEM((seqs, SUBLANES + rows, D_MODEL), F32),
            pltpu.VMEM((tile, D_MODEL), BF16),
            pltpu.VMEM((tile, D_MODEL), BF16),
            pltpu.VMEM((1, N_HEADS * rows, SCORE_COLS), F32),
        ],
        compiler_params=_params(1),
        name=f"mixer_sample_l{layer}",
    )(slopes, w["sink"], x2, w["g_mix"], kc, vc, st, *_mixer_weights(w))


def _ffn(layer, x2, mk, mv, w, groups, rows, final, name):
    total = x2.shape[0]
    tile = groups * rows
    steps_per_group = (total // mk.shape[1]) // rows if groups == 1 else 1
    mem_spec = pl.BlockSpec((None, groups, N_MEM, MEM_DIM), lambda i: (layer, i // steps_per_group, 0, 0))
    row = lambda i: (i, 0)
    return pl.pallas_call(
        functools.partial(_ffn_kernel, groups=groups, rows=rows, final=final),
        grid=(total // tile,),
        in_specs=[pl.BlockSpec((tile, D_MODEL), row), mem_spec, mem_spec,
                  _resident((1, D_MODEL), layer),
                  _resident((D_MODEL, MEM_DIM), layer),
                  _resident((MEM_DIM, D_MODEL), layer),
                  _resident((1, D_MODEL), layer),
                  _resident((D_MODEL, D_FF), layer),
                  _resident((D_FF, D_MODEL), layer),
                  pl.BlockSpec((1, D_MODEL), lambda i: (0, 0))],
        out_specs=pl.BlockSpec((tile, D_MODEL), row),
        out_shape=jax.ShapeDtypeStruct(x2.shape, F32),
        scratch_shapes=[
            pltpu.VMEM((tile, D_MODEL), BF16),
            pltpu.VMEM((tile, MEM_DIM), BF16),
            pltpu.VMEM((tile, MEM_DIM), BF16),
            pltpu.VMEM((tile, D_MODEL), F32),
            pltpu.VMEM((tile, D_FF), BF16),
        ],
        compiler_params=_params(1),
        name=name,
    )(x2, mk, mv, w["g_cross"], w["w_cq"], w["w_co"], w["g_mlp"], w["w_up"], w["w_down"], w["g_final"])


def _memory_kv(mem, g_mem, w_ckv):
    batch = mem.shape[0]
    out = jax.ShapeDtypeStruct((DEPTH, batch, N_MEM, MEM_DIM), F32)
    return pl.pallas_call(
        _memkv_kernel,
        grid=(DEPTH, batch),
        in_specs=[pl.BlockSpec((None, N_MEM, D_MODEL), lambda l, b: (b, 0, 0)),
                  pl.BlockSpec((None, 1, D_MODEL), lambda l, b: (l, 0, 0)),
                  pl.BlockSpec((None, D_MODEL, 2 * MEM_DIM), lambda l, b: (l, 0, 0))],
        out_specs=[pl.BlockSpec((None, None, N_MEM, MEM_DIM), lambda l, b: (l, b, 0, 0))] * 2,
        out_shape=[out, out],
        compiler_params=_params(2),
        name="memory_kv",
    )(mem, g_mem, w_ckv)


def kernel(x_prompt, x_sample, mem_prompt, cache_attn_k, cache_attn_v, state_conv, cache_mem_k, cache_mem_v,
           norm_mix_g, w_in, conv_w, attn_sink, w_attn_out, w_conv_out, w_mix_out, norm_cross_g, norm_mem_g,
           w_cq, w_ckv, w_co, norm_mlp_g, w_up, w_down, norm_final_g):
    batch, seq, _ = x_prompt.shape
    dec_batch, dec_seq, _ = x_sample.shape
    assert seq % PROMPT_TILE == 0 and PROMPT_TILE % CHUNK == 0 and PROMPT_TILE >= WINDOW
    assert dec_seq <= CHUNK and dec_seq % 16 == 0 and dec_batch % SAMPLE_SEQS == 0
    assert cache_attn_k.shape[2] == WINDOW

    n_q = N_HEADS * HEAD_DIM
    n_conv = n_q + 2 * KV_DIM
    n_gate = n_conv + 3 * D_MODEL
    w = {
        "sink": attn_sink,
        "g_mix": norm_mix_g[:, None, :],
        "w_qkv": w_in[:, :, :n_conv].astype(BF16),
        "w_conv": w_in[:, :, n_conv:n_gate].astype(BF16),
        "w_gate": w_in[:, :, n_gate:].astype(BF16),
        "conv_w": conv_w,
        "w_attn_out": w_attn_out.astype(BF16),
        "w_conv_out": w_conv_out.astype(BF16),
        "w_mix_out": w_mix_out.astype(BF16),
        "g_cross": norm_cross_g[:, None, :],
        "w_cq": w_cq.astype(BF16),
        "w_co": w_co.astype(BF16),
        "g_mlp": norm_mlp_g[:, None, :],
        "w_up": w_up.astype(BF16),
        "w_down": w_down.astype(BF16),
        "g_final": norm_final_g[None, :],
    }
    slopes = jnp.exp2(-8.0 * jnp.arange(1, N_HEADS + 1, dtype=F32) / N_HEADS)

    mk_p, mv_p = _memory_kv(mem_prompt, norm_mem_g[:, None, :], w_ckv.astype(BF16))

    kc = cache_attn_k.reshape(DEPTH, dec_batch, WINDOW, KV_DIM)
    vc = cache_attn_v.reshape(DEPTH, dec_batch, WINDOW, KV_DIM)
    mk_s = cache_mem_k.reshape(DEPTH, dec_batch, N_MEM, MEM_DIM)
    mv_s = cache_mem_v.reshape(DEPTH, dec_batch, N_MEM, MEM_DIM)

    xp = x_prompt
    xs = x_sample.reshape(dec_batch * dec_seq, D_MODEL)
    kp_l, vp_l, cp_l, ks_l, vs_l, cs_l = [], [], [], [], [], []
    for layer in range(DEPTH):
        final = layer == DEPTH - 1
        xp, kp, vp, cp = _mixer_prompt(layer, xp, slopes, w)
        xp = _ffn(layer, xp.reshape(batch * seq, D_MODEL), mk_p, mv_p, w, 1, PROMPT_TILE, final,
                  f"ffn_prompt_l{layer}").reshape(batch, seq, D_MODEL)
        xs, kn, vn, cs = _mixer_sample(layer, xs, slopes, kc, vc, state_conv, w, dec_seq)
        xs = _ffn(layer, xs, mk_s, mv_s, w, SAMPLE_SEQS, dec_seq, final, f"ffn_sample_l{layer}")
        kp_l.append(kp)
        vp_l.append(vp)
        cp_l.append(cp)
        ks_l.append(kn)
        vs_l.append(vn)
        cs_l.append(cs)

    kv_p = (DEPTH, batch, WINDOW, N_KV_HEADS, HEAD_DIM)
    kv_s = (DEPTH, dec_batch, dec_seq, N_KV_HEADS, HEAD_DIM)
    mem_shape = (DEPTH, batch, N_MEM, MEM_HEADS, MEM_DIM // MEM_HEADS)
    return (xp, xs.reshape(dec_batch, dec_seq, D_MODEL),
            jnp.stack(kp_l).reshape(kv_p), jnp.stack(vp_l).reshape(kv_p), jnp.stack(cp_l),
            mk_p.reshape(mem_shape), mv_p.reshape(mem_shape),
            jnp.stack(ks_l).reshape(kv_s), jnp.stack(vs_l).reshape(kv_s), jnp.stack(cs_l))
```

```python
import functools

import jax
import jax.numpy as jnp
from jax import lax
from jax.experimental import pallas as pl
from jax.experimental.pallas import tpu as pltpu

D_MODEL = 1024
DEPTH = 4
CHUNK = 64
WINDOW = 128
WIN_CHUNKS = WINDOW // CHUNK
HEAD_DIM = 64
N_HEADS = 16
N_KV_HEADS = 4
GROUP = N_HEADS // N_KV_HEADS
KV_DIM = N_KV_HEADS * HEAD_DIM
N_MEM = 256
MEM_HEADS = 4
MEM_DIM = 256
D_FF = 4 * D_MODEL
EPS = 1e-6
ATTN_SCALE = HEAD_DIM ** -0.5
MEM_SCALE = (MEM_DIM // MEM_HEADS) ** -0.5
CONV_TAPS = 3
CONV_LEFT = CONV_TAPS - 1

LANES = 128
SUBLANES = 8
HALF = LANES // 2
SCORE_COLS = 2 * LANES
MASKED = -1e30
V7X_VMEM_BYTES = 64 * 1024 * 1024
VMEM_LIMIT = V7X_VMEM_BYTES - 8 * 1024 * 1024

PROMPT_TILE = 256
SAMPLE_SEQS = 8
COLS = 256

BF16 = jnp.bfloat16
F32 = jnp.float32


def _dot(a, b):
    return jnp.dot(a, b, preferred_element_type=F32)


def _dot_nt(a, b):
    return lax.dot_general(a, b, (((1,), (1,)), ((), ())), preferred_element_type=F32)


def _rms(x, g):
    return x * lax.rsqrt(jnp.mean(x * x, axis=-1, keepdims=True) + EPS) * g


def _low_half(shape):
    return lax.broadcasted_iota(jnp.int32, shape, len(shape) - 1) < HALF


def _dup_halves(col):
    swapped = pltpu.roll(col, HALF, axis=1)
    lo = _low_half(col.shape)
    return jnp.where(lo, col, swapped), jnp.where(lo, swapped, col)


def _store_lane_tiled(dst_ref, kv, store):
    for c in range(KV_DIM // LANES):
        even, odd = _dup_halves(kv[:, c * LANES:(c + 1) * LANES])
        store(dst_ref, 2 * c, even)
        store(dst_ref, 2 * c + 1, odd)


def _pad_keys(block):
    pad = jnp.zeros((SCORE_COLS - block.shape[0], block.shape[1]), block.dtype)
    return jnp.concatenate([block, pad], axis=0)


def _head_rows(qa, qb):
    lo = _low_half(qa.shape)
    zero = jnp.zeros_like(qa)
    blocks = (
        jnp.concatenate([jnp.where(lo, qa, zero), zero], axis=1),
        jnp.concatenate([jnp.where(lo, zero, qa), zero], axis=1),
        jnp.concatenate([zero, jnp.where(lo, qb, zero)], axis=1),
        jnp.concatenate([zero, jnp.where(lo, zero, qb)], axis=1),
    )
    return jnp.concatenate(blocks, axis=0)


def _pick_heads(o4, rows):
    lo = _low_half((rows, LANES))
    oa = jnp.where(lo, o4[0:rows, 0:LANES], o4[rows:2 * rows, 0:LANES])
    ob = jnp.where(lo, o4[2 * rows:3 * rows, LANES:], o4[3 * rows:4 * rows, LANES:])
    return oa, ob


def _build_score_bias(bias_ref, slopes_ref, sink_ref, layer, nq, nk, first_valid_keys):
    qi = lax.broadcasted_iota(jnp.int32, (nq, SCORE_COLS), 0)
    ji = lax.broadcasted_iota(jnp.int32, (nq, SCORE_COLS), 1)
    dist = jnp.abs(qi + WINDOW - ji).astype(F32)
    for h in range(N_HEADS):
        base = jnp.where(ji == nk, sink_ref[layer, h], -slopes_ref[h] * dist)
        for v, first in enumerate(first_valid_keys):
            valid = (ji >= first) & (ji <= nk)
            bias_ref[v, h * nq:(h + 1) * nq, :] = jnp.where(valid, base, MASKED)


def _softmax(s):
    p = jnp.exp(s - jnp.max(s, axis=-1, keepdims=True))
    return p * (1.0 / jnp.sum(p, axis=-1, keepdims=True))


def _attend_group(q_s, a_s, r0, nq, kv_head, kblock, vblock, bias):
    c0 = kv_head * GROUP * HEAD_DIM
    qs = _head_rows(q_s[pl.ds(r0, nq), c0:c0 + LANES], q_s[pl.ds(r0, nq), c0 + LANES:c0 + 2 * LANES])
    p = _softmax(_dot_nt(qs, _pad_keys(kblock)) + bias)
    oa, ob = _pick_heads(_dot(p.astype(BF16), _pad_keys(vblock)), nq)
    a_s[pl.ds(r0, nq), c0:c0 + LANES] = oa.astype(BF16)
    a_s[pl.ds(r0, nq), c0 + LANES:c0 + 2 * LANES] = ob.astype(BF16)


def _q_chunk(c, h_s, q_s, wqkv_ref, scale):
    q_s[:, c:c + COLS] = (_dot(h_s[...], wqkv_ref[:, c:c + COLS]) * scale).astype(BF16)


def _project_kv(h_s, wqkv_ref):
    hb = h_s[...]
    k = _dot(hb, wqkv_ref[:, D_MODEL:D_MODEL + KV_DIM])
    v = _dot(hb, wqkv_ref[:, D_MODEL + KV_DIM:D_MODEL + 2 * KV_DIM])
    return k, v


def _gate_chunk(c, h_s, gate_s, wgate_ref):
    hb = h_s[...]
    gate_s[0, :, c:c + COLS] = jax.nn.sigmoid(_dot(hb, wgate_ref[:, c:c + COLS]))
    gate_s[1, :, c:c + COLS] = jax.nn.sigmoid(_dot(hb, wgate_ref[:, D_MODEL + c:D_MODEL + c + COLS]))


def _conv_chunk(c, h_s, boin_s, u3, rows, co_ref, wconv_ref, convw_ref, next_left_ref=None):
    groups = u3.shape[0]
    total = groups * rows
    cs = slice(c, c + COLS)
    hb = h_s[...]
    ch = _dot(hb, wconv_ref[:, c:c + COLS])
    cc = _dot(hb, wconv_ref[:, 2 * D_MODEL + c:2 * D_MODEL + c + COLS])
    u = cc * ch
    u3[:, SUBLANES:SUBLANES + rows, cs] = u.reshape(groups, rows, COLS)
    um2 = u3[:, SUBLANES - 2:SUBLANES - 2 + rows, cs].reshape(total, COLS)
    um1 = u3[:, SUBLANES - 1:SUBLANES - 1 + rows, cs].reshape(total, COLS)
    conv = um2 * convw_ref[0:1, cs] + um1 * convw_ref[1:2, cs] + u * convw_ref[2:3, cs]
    cb = _dot(hb, wconv_ref[:, D_MODEL + c:D_MODEL + c + COLS])
    boin_s[:, cs] = (cb * conv).astype(BF16)
    tail = u3[:, SUBLANES + rows - CONV_LEFT:SUBLANES + rows, cs]
    co_ref[:, :, cs] = tail
    if next_left_ref is not None:
        next_left_ref[:, SUBLANES - CONV_LEFT:SUBLANES, cs] = tail


def _merge(a_s, boin_s, gate_s, mix_s, x_ref, xo_ref, wao_ref, wco_ref, wmo_ref):
    for c in range(0, D_MODEL, COLS):
        cs = slice(c, c + COLS)
        a = _dot(a_s[...], wao_ref[:, cs])
        bo = _dot(boin_s[...], wco_ref[:, cs])
        mix_s[:, cs] = (gate_s[0, :, cs] * a + gate_s[1, :, cs] * bo).astype(BF16)
    for c in range(0, D_MODEL, COLS):
        cs = slice(c, c + COLS)
        xo_ref[:, cs] = x_ref[:, cs] + _dot(mix_s[...], wmo_ref[:, cs])


BAND_HEAD_ORDER = (0, 2, 1, 3)
BAND_KEYS = WINDOW + CHUNK
START_VARIANTS = WIN_CHUNKS + 1
BLOCKS_PER_STAGE = 2
LOG2E = 1.4426950408889634
BAND_Q_SCALE = ATTN_SCALE * LOG2E


def _build_band_bias(bias_ref, slopes_ref):
    ji = lax.broadcasted_iota(jnp.int32, (BAND_KEYS, 2 * LANES), 0)
    li = lax.broadcasted_iota(jnp.int32, (BAND_KEYS, 2 * LANES), 1)
    dist = jnp.abs(li % CHUNK + WINDOW - ji).astype(F32)
    slot = li // CHUNK
    for kh in range(N_KV_HEADS):
        slope = jnp.zeros((BAND_KEYS, 2 * LANES), F32)
        for s, g in enumerate(BAND_HEAD_ORDER):
            slope = jnp.where(slot == s, slopes_ref[kh * GROUP + g], slope)
        base = -slope * dist * LOG2E
        rows = slice(kh * BAND_KEYS, (kh + 1) * BAND_KEYS)
        for v in range(START_VARIANTS):
            bias_ref[v, rows, :] = jnp.where(ji >= (WIN_CHUNKS - v) * CHUNK, base, MASKED)


def _band_query_rows(qa, qb):
    lo = _low_half(qa.shape)
    zero = jnp.zeros_like(qa)
    blocks = (
        jnp.concatenate([jnp.where(lo, qa, zero), zero], axis=1),
        jnp.concatenate([zero, jnp.where(lo, qb, zero)], axis=1),
        jnp.concatenate([jnp.where(lo, zero, qa), zero], axis=1),
        jnp.concatenate([zero, jnp.where(lo, zero, qb)], axis=1),
    )
    return jnp.concatenate(blocks, axis=0)


def _band_scores(q_s, r0, kv_head, kblock, bias):
    c0 = kv_head * GROUP * HEAD_DIM
    qs = _band_query_rows(q_s[r0:r0 + CHUNK, c0:c0 + LANES], q_s[r0:r0 + CHUNK, c0 + LANES:c0 + 2 * LANES])
    return _dot_nt(kblock, qs) + bias


def _band_softmax(st, kv_head, sink_of):
    lo = _low_half((1, LANES))
    probs, inv_den = [], []
    for col in range(2):
        s = st[:, col * LANES:(col + 1) * LANES]
        ha, hb = BAND_HEAD_ORDER[2 * col], BAND_HEAD_ORDER[2 * col + 1]
        sink = jnp.where(lo, sink_of(kv_head * GROUP + ha), sink_of(kv_head * GROUP + hb)) * LOG2E
        m = jnp.maximum(jnp.max(s, axis=0, keepdims=True), sink)
        p = jnp.exp2(s - m)
        inv_den.append(1.0 / (jnp.sum(p, axis=0, keepdims=True) + jnp.exp2(sink - m)))
        probs.append(p.astype(BF16))
    return jnp.concatenate(probs, axis=1), inv_den


def _band_values(vt_block, lead, pt):
    parts = [pt]
    if lead:
        parts.insert(0, jnp.zeros((lead, 2 * LANES), BF16))
    if vt_block.shape[1] > lead + BAND_KEYS:
        parts.append(jnp.zeros((vt_block.shape[1] - lead - BAND_KEYS, 2 * LANES), BF16))
    return _dot(vt_block, jnp.concatenate(parts, axis=0))


def _band_store(a_s, r0, kv_head, ot, inv_den):
    c0 = kv_head * GROUP * HEAD_DIM
    stacked = jnp.concatenate([ot[:, 0:LANES] * inv_den[0], ot[:, LANES:] * inv_den[1]], axis=0)
    out = stacked.T.astype(BF16)
    a_s[r0:r0 + CHUNK, c0:c0 + LANES] = out[0:CHUNK]
    a_s[r0:r0 + CHUNK, c0 + LANES:c0 + 2 * LANES] = out[CHUNK:]


def _mixer_prompt_kernel(slopes_ref, sink_ref, x_ref, g_ref, wqkv_ref, wconv_ref, wgate_ref, convw_ref,
                         wao_ref, wco_ref, wmo_ref,
                         xo_ref, ko_ref, vo_ref, co_ref,
                         h_s, q_s, k4_s, vt_s, a_s, u_s, boin_s, gate_s, mix_s, bias_s,
                         kprev_s, vtprev_s, uprev_s, *, layer, tile):
    t = pl.program_id(1)
    cur = t % 2
    nxt = 1 - cur

    @pl.when(t == 0)
    def _():
        _build_band_bias(bias_s, slopes_ref)
        kprev_s[cur] = jnp.zeros(kprev_s.shape[1:], BF16)
        vtprev_s[cur] = jnp.zeros(vtprev_s.shape[1:], BF16)
        uprev_s[cur] = jnp.zeros(uprev_s.shape[1:], F32)

    k4_s[:, 0:WINDOW, :] = kprev_s[cur]
    vt_s[:, :, 0:WINDOW] = vtprev_s[cur]
    left = slice(SUBLANES - CONV_LEFT, SUBLANES)
    u_s[:, left, :] = uprev_s[cur, :, left, :]

    h_s[...] = _rms(x_ref[...], g_ref[...]).astype(BF16)
    k, v = _project_kv(h_s, wqkv_ref)
    ko_ref[...] = k[tile - WINDOW:, :]
    vo_ref[...] = v[tile - WINDOW:, :]

    def store_new(dst_ref, head, dup):
        dup = dup.astype(BF16)
        dst_ref[head, WINDOW:WINDOW + tile, 0:LANES] = dup
        dst_ref[head, WINDOW:WINDOW + tile, LANES:] = dup

    _store_lane_tiled(k4_s, k, store_new)
    vt = v.T.astype(BF16)
    for kh in range(N_KV_HEADS):
        vt_s[kh, :, WINDOW:WINDOW + tile] = vt[kh * HEAD_DIM:(kh + 1) * HEAD_DIM, :]

    cols = range(0, D_MODEL, COLS)
    side = [(COLS, functools.partial(_q_chunk, c, h_s, q_s, wqkv_ref, BAND_Q_SCALE)) for c in cols[1:]]
    side += [(3 * COLS, functools.partial(_conv_chunk, c, h_s, boin_s, u_s, tile, co_ref, wconv_ref, convw_ref,
                                          uprev_s.at[nxt])) for c in cols]
    side += [(2 * COLS, functools.partial(_gate_chunk, c, h_s, gate_s, wgate_ref)) for c in cols]
    blocks = [(kh, r0) for kh in range(N_KV_HEADS) for r0 in range(0, tile, CHUNK)]
    stages = [blocks[i:i + BLOCKS_PER_STAGE] for i in range(0, len(blocks), BLOCKS_PER_STAGE)]
    side_per_stage = sum(cost for cost, _ in side) / len(stages)

    sink_of = lambda h: sink_ref[layer, h]

    def scores(kh, r0):
        chunk = r0 // CHUNK
        variant = jnp.where(t == 0, chunk, WIN_CHUNKS) if chunk < WIN_CHUNKS else WIN_CHUNKS
        bias = bias_s[variant, kh * BAND_KEYS:(kh + 1) * BAND_KEYS, :]
        return _band_scores(q_s, r0, kh, k4_s[kh, r0:r0 + BAND_KEYS, :], bias)

    def values(kh, r0, pt):
        lead = r0 % LANES
        return _band_values(vt_s[kh, :, r0 - lead:r0 - lead + 2 * LANES], lead, pt)

    _q_chunk(0, h_s, q_s, wqkv_ref, BAND_Q_SCALE)
    emitted = 0
    st_next = [scores(*b) for b in stages[0]]
    for i, stage in enumerate(stages):
        st = st_next
        while side and emitted < (i + 1) * side_per_stage and side[0][0] == COLS:
            cost, emit = side.pop(0)
            emit()
            emitted += cost
        if i + 1 < len(stages):
            st_next = [scores(*b) for b in stages[i + 1]]
        soft = [_band_softmax(s, kh, sink_of) for s, (kh, _) in zip(st, stage)]
        while side and emitted < (i + 1) * side_per_stage:
            cost, emit = side.pop(0)
            emit()
            emitted += cost
        outs = [values(kh, r0, pt) for (pt, _), (kh, r0) in zip(soft, stage)]
        for ot, (_, inv_den), (kh, r0) in zip(outs, soft, stage):
            _band_store(a_s, r0, kh, ot, inv_den)
    for _, emit in side:
        emit()

    kprev_s[nxt] = k4_s[:, tile:tile + WINDOW, :]
    vtprev_s[nxt] = vt_s[:, :, tile:tile + WINDOW]

    _merge(a_s, boin_s, gate_s, mix_s, x_ref, xo_ref, wao_ref, wco_ref, wmo_ref)


def _mixer_sample_kernel(slopes_ref, sink_ref, x_ref, g_ref, kc_ref, vc_ref, st_ref,
                         wqkv_ref, wconv_ref, wgate_ref, convw_ref, wao_ref, wco_ref, wmo_ref,
                         xo_ref, ko_ref, vo_ref, co_ref,
                         h_s, q_s, k4_s, v4_s, a_s, u_s, boin_s, gate_s, mix_s, bias_s, *, layer, seqs, rows):
    keys = WINDOW + rows

    @pl.when(pl.program_id(0) == 0)
    def _():
        _build_score_bias(bias_s, slopes_ref, sink_ref, layer, rows, keys, (0,))

    u_s[:, SUBLANES - CONV_LEFT:SUBLANES, :] = st_ref[...]

    h_s[...] = _rms(x_ref[...], g_ref[...]).astype(BF16)
    for c in range(0, D_MODEL, COLS):
        _q_chunk(c, h_s, q_s, wqkv_ref, ATTN_SCALE)
    k, v = _project_kv(h_s, wqkv_ref)
    ko_ref[...] = k
    vo_ref[...] = v

    def store_cache(dst_ref, head, dup):
        dup = dup.reshape(seqs, WINDOW, LANES).astype(BF16)
        dst_ref[head, :, 0:WINDOW, 0:LANES] = dup
        dst_ref[head, :, 0:WINDOW, LANES:] = dup

    def store_new(dst_ref, head, dup):
        dup = dup.reshape(seqs, rows, LANES).astype(BF16)
        dst_ref[head, :, WINDOW:keys, 0:LANES] = dup
        dst_ref[head, :, WINDOW:keys, LANES:] = dup

    _store_lane_tiled(k4_s, kc_ref[...].reshape(seqs * WINDOW, KV_DIM), store_cache)
    _store_lane_tiled(v4_s, vc_ref[...].reshape(seqs * WINDOW, KV_DIM), store_cache)
    _store_lane_tiled(k4_s, k, store_new)
    _store_lane_tiled(v4_s, v, store_new)

    def attend(g, carry):
        r0 = pl.multiple_of(g * rows, rows)
        for kh in range(N_KV_HEADS):
            bias = bias_s[0, kh * GROUP * rows:(kh + 1) * GROUP * rows, :]
            _attend_group(q_s, a_s, r0, rows, kh, k4_s[kh, g], v4_s[kh, g], bias)
        return carry

    lax.fori_loop(0, seqs, attend, 0)

    for c in range(0, D_MODEL, COLS):
        _conv_chunk(c, h_s, boin_s, u_s, rows, co_ref, wconv_ref, convw_ref)
        _gate_chunk(c, h_s, gate_s, wgate_ref)
    _merge(a_s, boin_s, gate_s, mix_s, x_ref, xo_ref, wao_ref, wco_ref, wmo_ref)


def _ffn_kernel(x_ref, mk_ref, mv_ref, gc_ref, wcq_ref, wcout_ref, gm_ref, wup_ref, wdown_ref, gf_ref,
                xo_ref,
                h_s, cq_s, co_s, x1_s, act_s, *, groups, rows, final):
    h_s[...] = _rms(x_ref[...], gc_ref[...]).astype(BF16)
    cq_s[...] = (_dot(h_s[...], wcq_ref[...]) * MEM_SCALE).astype(BF16)
    lo = _low_half((rows, LANES))
    zero = jnp.zeros((rows, LANES), BF16)
    for g in range(groups):
        rs = slice(g * rows, (g + 1) * rows)
        mk = mk_ref[g].astype(BF16)
        mv = mv_ref[g].astype(BF16)
        if MEM_HEADS * rows <= 2 * LANES:
            qs = _head_rows(cq_s[rs, 0:LANES], cq_s[rs, LANES:])
            oa, ob = _pick_heads(_dot(_softmax(_dot_nt(qs, mk)).astype(BF16), mv), rows)
            co_s[rs, 0:LANES] = oa.astype(BF16)
            co_s[rs, LANES:] = ob.astype(BF16)
            continue
        outs = []
        for hh in range(MEM_HEADS):
            c, odd = divmod(hh, 2)
            qcol = cq_s[rs, c * LANES:(c + 1) * LANES]
            qm = jnp.where(lo, zero, qcol) if odd else jnp.where(lo, qcol, zero)
            qm = jnp.concatenate([qm, zero] if c == 0 else [zero, qm], axis=1)
            p = _softmax(_dot_nt(qm, mk))
            outs.append(_dot(p.astype(BF16), mv)[:, c * LANES:(c + 1) * LANES])
        co_s[rs, 0:LANES] = jnp.where(lo, outs[0], outs[1]).astype(BF16)
        co_s[rs, LANES:] = jnp.where(lo, outs[2], outs[3]).astype(BF16)
    for c in range(0, D_MODEL, COLS):
        cs = slice(c, c + COLS)
        x1_s[:, cs] = x_ref[:, cs] + _dot(co_s[...], wcout_ref[:, cs])
    h_s[...] = _rms(x1_s[...], gm_ref[...]).astype(BF16)
    for c in range(0, D_FF, COLS):
        up = jnp.maximum(_dot(h_s[...], wup_ref[:, c:c + COLS]), 0.0)
        act_s[:, c:c + COLS] = (up * up).astype(BF16)
    for c in range(0, D_MODEL, COLS):
        cs = slice(c, c + COLS)
        x1_s[:, cs] = x1_s[:, cs] + _dot(act_s[...], wdown_ref[:, cs])
    if final:
        xo_ref[...] = _rms(x1_s[...], gf_ref[...])
    else:
        xo_ref[...] = x1_s[...]


def _memkv_kernel(mem_ref, g_ref, w_ref, mk_ref, mv_ref):
    h = _rms(mem_ref[...], g_ref[...]).astype(BF16)
    kv = _dot(h, w_ref[...])
    mk_ref[...] = kv[:, 0:MEM_DIM]
    mv_ref[...] = kv[:, MEM_DIM:]


def _params(n_axes):
    return pltpu.CompilerParams(dimension_semantics=("arbitrary",) * n_axes, vmem_limit_bytes=VMEM_LIMIT)


def _resident(shape, layer):
    zeros = (0,) * len(shape)
    return pl.BlockSpec((None,) + tuple(shape), lambda *_: (layer,) + zeros, pipeline_mode=pl.Buffered(1))


def _smem():
    return pl.BlockSpec(memory_space=pltpu.SMEM)


def _mixer_weight_specs(layer):
    return [
        _resident((D_MODEL, D_MODEL + 2 * KV_DIM), layer),
        _resident((D_MODEL, 3 * D_MODEL), layer),
        _resident((D_MODEL, 2 * D_MODEL), layer),
        _resident((CONV_TAPS, D_MODEL), layer),
        _resident((D_MODEL, D_MODEL), layer),
        _resident((D_MODEL, D_MODEL), layer),
        _resident((D_MODEL, D_MODEL), layer),
    ]


def _mixer_weights(w):
    return (w["w_qkv"], w["w_conv"], w["w_gate"], w["conv_w"], w["w_attn_out"], w["w_conv_out"], w["w_mix_out"])


def _mixer_prompt(layer, x, slopes, w):
    batch, seq, _ = x.shape
    tile = PROMPT_TILE
    row = lambda b, t: (b, t, 0)
    per_seq = lambda b, t: (b, 0, 0)
    return pl.pallas_call(
        functools.partial(_mixer_prompt_kernel, layer=layer, tile=tile),
        grid=(batch, seq // tile),
        in_specs=[_smem(), _smem(),
                  pl.BlockSpec((None, tile, D_MODEL), row),
                  _resident((1, D_MODEL), layer)] + _mixer_weight_specs(layer),
        out_specs=[pl.BlockSpec((None, tile, D_MODEL), row),
                   pl.BlockSpec((None, WINDOW, KV_DIM), per_seq),
                   pl.BlockSpec((None, WINDOW, KV_DIM), per_seq),
                   pl.BlockSpec((1, CONV_LEFT, D_MODEL), per_seq)],
        out_shape=[jax.ShapeDtypeStruct(x.shape, F32),
                   jax.ShapeDtypeStruct((batch, WINDOW, KV_DIM), F32),
                   jax.ShapeDtypeStruct((batch, WINDOW, KV_DIM), F32),
                   jax.ShapeDtypeStruct((batch, CONV_LEFT, D_MODEL), F32)],
        scratch_shapes=[
            pltpu.VMEM((tile, D_MODEL), BF16),
            pltpu.VMEM((tile, D_MODEL), BF16),
            pltpu.VMEM((N_KV_HEADS, WINDOW + tile, 2 * LANES), BF16),
            pltpu.VMEM((N_KV_HEADS, HEAD_DIM, WINDOW + tile), BF16),
            pltpu.VMEM((tile, D_MODEL), BF16),
            pltpu.VMEM((1, SUBLANES + tile, D_MODEL), F32),
            pltpu.VMEM((tile, D_MODEL), BF16),
            pltpu.VMEM((2, tile, D_MODEL), F32),
            pltpu.VMEM((tile, D_MODEL), BF16),
            pltpu.VMEM((START_VARIANTS, N_KV_HEADS * BAND_KEYS, 2 * LANES), F32),
            pltpu.VMEM((2, N_KV_HEADS, WINDOW, 2 * LANES), BF16),
            pltpu.VMEM((2, N_KV_HEADS, HEAD_DIM, WINDOW), BF16),
            pltpu.VMEM((2, 1, SUBLANES, D_MODEL), F32),
        ],
        compiler_params=_params(2),
        name=f"mixer_prompt_l{layer}",
    )(slopes, w["sink"], x, w["g_mix"], *_mixer_weights(w))


def _mixer_sample(layer, x2, slopes, kc, vc, st, w, rows):
    total = x2.shape[0]
    seqs = SAMPLE_SEQS
    tile = seqs * rows
    keys = WINDOW + rows
    row = lambda i: (i, 0)
    per_seq = lambda i: (layer, i, 0, 0)
    return pl.pallas_call(
        functools.partial(_mixer_sample_kernel, layer=layer, seqs=seqs, rows=rows),
        grid=(total // tile,),
        in_specs=[_smem(), _smem(),
                  pl.BlockSpec((tile, D_MODEL), row),
                  _resident((1, D_MODEL), layer),
                  pl.BlockSpec((None, seqs, WINDOW, KV_DIM), per_seq),
                  pl.BlockSpec((None, seqs, WINDOW, KV_DIM), per_seq),
                  pl.BlockSpec((None, seqs, CONV_LEFT, D_MODEL), per_seq),
                  ] + _mixer_weight_specs(layer),
        out_specs=[pl.BlockSpec((tile, D_MODEL), row),
                   pl.BlockSpec((tile, KV_DIM), row),
                   pl.BlockSpec((tile, KV_DIM), row),
                   pl.BlockSpec((seqs, CONV_LEFT, D_MODEL), lambda i: (i, 0, 0))],
        out_shape=[jax.ShapeDtypeStruct(x2.shape, F32),
                   jax.ShapeDtypeStruct((total, KV_DIM), F32),
                   jax.ShapeDtypeStruct((total, KV_DIM), F32),
                   jax.ShapeDtypeStruct((total // rows, CONV_LEFT, D_MODEL), F32)],
        scratch_shapes=[
            pltpu.VMEM((tile, D_MODEL), BF16),
            pltpu.VMEM((tile, D_MODEL), BF16),
            pltpu.VMEM((N_KV_HEADS, seqs, keys, 2 * LANES), BF16),
            pltpu.VMEM((N_KV_HEADS, seqs, keys, 2 * LANES), BF16),
            pltpu.VMEM((tile, D_MODEL), BF16),
            pltpu.VMEM((seqs, SUBLANES + rows, D_MODEL), F32),
            pltpu.VMEM((tile, D_MODEL), BF16),
            pltpu.VMEM((2, tile, D_MODEL), F32),
            pltpu.VMEM((tile, D_MODEL), BF16),
            pltpu.VMEM((1, N_HEADS * rows, SCORE_COLS), F32),
        ],
        compiler_params=_params(1),
        name=f"mixer_sample_l{layer}",
    )(slopes, w["sink"], x2, w["g_mix"], kc, vc, st, *_mixer_weights(w))


def _ffn(layer, x2, mk, mv, w, groups, rows, final, name):
    total = x2.shape[0]
    tile = groups * rows
    steps_per_group = (total // mk.shape[1]) // rows if groups == 1 else 1
    mem_spec = pl.BlockSpec((None, groups, N_MEM, MEM_DIM), lambda i: (layer, i // steps_per_group, 0, 0))
    row = lambda i: (i, 0)
    return pl.pallas_call(
        functools.partial(_ffn_kernel, groups=groups, rows=rows, final=final),
        grid=(total // tile,),
        in_specs=[pl.BlockSpec((tile, D_MODEL), row), mem_spec, mem_spec,
                  _resident((1, D_MODEL), layer),
                  _resident((D_MODEL, MEM_DIM), layer),
                  _resident((MEM_DIM, D_MODEL), layer),
                  _resident((1, D_MODEL), layer),
                  _resident((D_MODEL, D_FF), layer),
                  _resident((D_FF, D_MODEL), layer),
                  pl.BlockSpec((1, D_MODEL), lambda i: (0, 0))],
        out_specs=pl.BlockSpec((tile, D_MODEL), row),
        out_shape=jax.ShapeDtypeStruct(x2.shape, F32),
        scratch_shapes=[
            pltpu.VMEM((tile, D_MODEL), BF16),
            pltpu.VMEM((tile, MEM_DIM), BF16),
            pltpu.VMEM((tile, MEM_DIM), BF16),
            pltpu.VMEM((tile, D_MODEL), F32),
            pltpu.VMEM((tile, D_FF), BF16),
        ],
        compiler_params=_params(1),
        name=name,
    )(x2, mk, mv, w["g_cross"], w["w_cq"], w["w_co"], w["g_mlp"], w["w_up"], w["w_down"], w["g_final"])


def _memory_kv(mem, g_mem, w_ckv):
    batch = mem.shape[0]
    out = jax.ShapeDtypeStruct((DEPTH, batch, N_MEM, MEM_DIM), F32)
    return pl.pallas_call(
        _memkv_kernel,
        grid=(DEPTH, batch),
        in_specs=[pl.BlockSpec((None, N_MEM, D_MODEL), lambda l, b: (b, 0, 0)),
                  pl.BlockSpec((None, 1, D_MODEL), lambda l, b: (l, 0, 0)),
                  pl.BlockSpec((None, D_MODEL, 2 * MEM_DIM), lambda l, b: (l, 0, 0))],
        out_specs=[pl.BlockSpec((None, None, N_MEM, MEM_DIM), lambda l, b: (l, b, 0, 0))] * 2,
        out_shape=[out, out],
        compiler_params=_params(2),
        name="memory_kv",
    )(mem, g_mem, w_ckv)


def kernel(x_prompt, x_sample, mem_prompt, cache_attn_k, cache_attn_v, state_conv, cache_mem_k, cache_mem_v,
           norm_mix_g, w_in, conv_w, attn_sink, w_attn_out, w_conv_out, w_mix_out, norm_cross_g, norm_mem_g,
           w_cq, w_ckv, w_co, norm_mlp_g, w_up, w_down, norm_final_g):
    batch, seq, _ = x_prompt.shape
    dec_batch, dec_seq, _ = x_sample.shape
    assert seq % PROMPT_TILE == 0 and PROMPT_TILE % LANES == 0 and PROMPT_TILE >= WINDOW
    assert dec_seq <= CHUNK and dec_seq % 16 == 0 and dec_batch % SAMPLE_SEQS == 0
    assert cache_attn_k.shape[2] == WINDOW

    n_q = N_HEADS * HEAD_DIM
    n_conv = n_q + 2 * KV_DIM
    n_gate = n_conv + 3 * D_MODEL
    w = {
        "sink": attn_sink,
        "g_mix": norm_mix_g[:, None, :],
        "w_qkv": w_in[:, :, :n_conv].astype(BF16),
        "w_conv": w_in[:, :, n_conv:n_gate].astype(BF16),
        "w_gate": w_in[:, :, n_gate:].astype(BF16),
        "conv_w": conv_w,
        "w_attn_out": w_attn_out.astype(BF16),
        "w_conv_out": w_conv_out.astype(BF16),
        "w_mix_out": w_mix_out.astype(BF16),
        "g_cross": norm_cross_g[:, None, :],
        "w_cq": w_cq.astype(BF16),
        "w_co": w_co.astype(BF16),
        "g_mlp": norm_mlp_g[:, None, :],
        "w_up": w_up.astype(BF16),
        "w_down": w_down.astype(BF16),
        "g_final": norm_final_g[None, :],
    }
    slopes = jnp.exp2(-8.0 * jnp.arange(1, N_HEADS + 1, dtype=F32) / N_HEADS)

    mk_p, mv_p = _memory_kv(mem_prompt, norm_mem_g[:, None, :], w_ckv.astype(BF16))

    kc = cache_attn_k.reshape(DEPTH, dec_batch, WINDOW, KV_DIM)
    vc = cache_attn_v.reshape(DEPTH, dec_batch, WINDOW, KV_DIM)
    mk_s = cache_mem_k.reshape(DEPTH, dec_batch, N_MEM, MEM_DIM)
    mv_s = cache_mem_v.reshape(DEPTH, dec_batch, N_MEM, MEM_DIM)

    xp = x_prompt
    xs = x_sample.reshape(dec_batch * dec_seq, D_MODEL)
    kp_l, vp_l, cp_l, ks_l, vs_l, cs_l = [], [], [], [], [], []
    for layer in range(DEPTH):
        final = layer == DEPTH - 1
        xp, kp, vp, cp = _mixer_prompt(layer, xp, slopes, w)
        xp = _ffn(layer, xp.reshape(batch * seq, D_MODEL), mk_p, mv_p, w, 1, PROMPT_TILE, final,
                  f"ffn_prompt_l{layer}").reshape(batch, seq, D_MODEL)
        xs, kn, vn, cs = _mixer_sample(layer, xs, slopes, kc, vc, state_conv, w, dec_seq)
        xs = _ffn(layer, xs, mk_s, mv_s, w, SAMPLE_SEQS, dec_seq, final, f"ffn_sample_l{layer}")
        kp_l.append(kp)
        vp_l.append(vp)
        cp_l.append(cp)
        ks_l.append(kn)
        vs_l.append(vn)
        cs_l.append(cs)

    kv_p = (DEPTH, batch, WINDOW, N_KV_HEADS, HEAD_DIM)
    kv_s = (DEPTH, dec_batch, dec_seq, N_KV_HEADS, HEAD_DIM)
    mem_shape = (DEPTH, batch, N_MEM, MEM_HEADS, MEM_DIM // MEM_HEADS)
    return (xp, xs.reshape(dec_batch, dec_seq, D_MODEL),
            jnp.stack(kp_l).reshape(kv_p), jnp.stack(vp_l).reshape(kv_p), jnp.stack(cp_l),
            mk_p.reshape(mem_shape), mv_p.reshape(mem_shape),
            jnp.stack(ks_l).reshape(kv_s), jnp.stack(vs_l).reshape(kv_s), jnp.stack(cs_l))
```

```python
import functools

import jax
import jax.numpy as jnp
from jax import lax
from jax.experimental import pallas as pl
from jax.experimental.pallas import tpu as pltpu

D_MODEL = 1024
DEPTH = 4
CHUNK = 64
WINDOW = 128
WIN_CHUNKS = WINDOW // CHUNK
HEAD_DIM = 64
N_HEADS = 16
N_KV_HEADS = 4
GROUP = N_HEADS // N_KV_HEADS
KV_DIM = N_KV_HEADS * HEAD_DIM
N_MEM = 256
MEM_HEADS = 4
MEM_DIM = 256
D_FF = 4 * D_MODEL
EPS = 1e-6
ATTN_SCALE = HEAD_DIM ** -0.5
MEM_SCALE = (MEM_DIM // MEM_HEADS) ** -0.5
CONV_TAPS = 3
CONV_LEFT = CONV_TAPS - 1

LANES = 128
SUBLANES = 8
HALF = LANES // 2
SCORE_COLS = 2 * LANES
MASKED = -1e30
V7X_VMEM_BYTES = 64 * 1024 * 1024
VMEM_LIMIT = V7X_VMEM_BYTES - 8 * 1024 * 1024

PROMPT_TILE = 256
SAMPLE_SEQS = 8
COLS = 256

BF16 = jnp.bfloat16
F32 = jnp.float32


def _dot(a, b):
    return jnp.dot(a, b, preferred_element_type=F32)


def _dot_nt(a, b):
    return lax.dot_general(a, b, (((1,), (1,)), ((), ())), preferred_element_type=F32)


def _rms(x, g):
    return x * lax.rsqrt(jnp.mean(x * x, axis=-1, keepdims=True) + EPS) * g


def _low_half(shape):
    return lax.broadcasted_iota(jnp.int32, shape, len(shape) - 1) < HALF


def _dup_halves(col):
    swapped = pltpu.roll(col, HALF, axis=1)
    lo = _low_half(col.shape)
    return jnp.where(lo, col, swapped), jnp.where(lo, swapped, col)


def _store_lane_tiled(dst_ref, kv, store):
    for c in range(KV_DIM // LANES):
        even, odd = _dup_halves(kv[:, c * LANES:(c + 1) * LANES])
        store(dst_ref, 2 * c, even)
        store(dst_ref, 2 * c + 1, odd)


def _pad_keys(block):
    pad = jnp.zeros((SCORE_COLS - block.shape[0], block.shape[1]), block.dtype)
    return jnp.concatenate([block, pad], axis=0)


def _head_rows(qa, qb):
    lo = _low_half(qa.shape)
    zero = jnp.zeros_like(qa)
    blocks = (
        jnp.concatenate([jnp.where(lo, qa, zero), zero], axis=1),
        jnp.concatenate([jnp.where(lo, zero, qa), zero], axis=1),
        jnp.concatenate([zero, jnp.where(lo, qb, zero)], axis=1),
        jnp.concatenate([zero, jnp.where(lo, zero, qb)], axis=1),
    )
    return jnp.concatenate(blocks, axis=0)


def _pick_heads(o4, rows):
    lo = _low_half((rows, LANES))
    oa = jnp.where(lo, o4[0:rows, 0:LANES], o4[rows:2 * rows, 0:LANES])
    ob = jnp.where(lo, o4[2 * rows:3 * rows, LANES:], o4[3 * rows:4 * rows, LANES:])
    return oa, ob


def _build_score_bias(bias_ref, slopes_ref, sink_ref, layer, nq, nk, first_valid_keys):
    qi = lax.broadcasted_iota(jnp.int32, (nq, SCORE_COLS), 0)
    ji = lax.broadcasted_iota(jnp.int32, (nq, SCORE_COLS), 1)
    dist = jnp.abs(qi + WINDOW - ji).astype(F32)
    for h in range(N_HEADS):
        base = jnp.where(ji == nk, sink_ref[layer, h], -slopes_ref[h] * dist)
        for v, first in enumerate(first_valid_keys):
            valid = (ji >= first) & (ji <= nk)
            bias_ref[v, h * nq:(h + 1) * nq, :] = jnp.where(valid, base, MASKED)


def _softmax(s):
    p = jnp.exp(s - jnp.max(s, axis=-1, keepdims=True))
    return p * (1.0 / jnp.sum(p, axis=-1, keepdims=True))


def _group_attention_unit(q_s, a_s, r0, nq, kv_head, kblock_of, vblock_of, bias_of):
    c0 = kv_head * GROUP * HEAD_DIM

    def scores():
        qs = _head_rows(q_s[r0:r0 + nq, c0:c0 + LANES], q_s[r0:r0 + nq, c0 + LANES:c0 + 2 * LANES])
        return _dot_nt(qs, _pad_keys(kblock_of())) + bias_of()

    def values(p):
        oa, ob = _pick_heads(_dot(p.astype(BF16), _pad_keys(vblock_of())), nq)
        a_s[r0:r0 + nq, c0:c0 + LANES] = oa.astype(BF16)
        a_s[r0:r0 + nq, c0 + LANES:c0 + 2 * LANES] = ob.astype(BF16)

    return scores, _softmax, values


def _emit_pipelined(units, side):
    slots = len(units) + 2
    total = sum(cost for cost, _ in side)
    emitted = 0
    state = {}
    for j in range(slots):
        if j < len(units):
            state[j] = units[j][0]()
        if 0 <= j - 1 < len(units):
            state[j - 1] = units[j - 1][1](state[j - 1])
        while side and emitted * slots < (j + 1) * total:
            cost, emit = side.pop(0)
            emit()
            emitted += cost
        if 0 <= j - 2 < len(units):
            units[j - 2][2](state.pop(j - 2))
    for _, emit in side:
        emit()


def _q_chunk(c, h_s, q_s, wqkv_ref, scale):
    q_s[:, c:c + COLS] = (_dot(h_s[...], wqkv_ref[:, c:c + COLS]) * scale).astype(BF16)


def _project_kv(h_s, wqkv_ref):
    hb = h_s[...]
    k = _dot(hb, wqkv_ref[:, D_MODEL:D_MODEL + KV_DIM])
    v = _dot(hb, wqkv_ref[:, D_MODEL + KV_DIM:D_MODEL + 2 * KV_DIM])
    return k, v


def _gate_chunk(c, h_s, gate_s, wgate_ref):
    hb = h_s[...]
    gate_s[0, :, c:c + COLS] = jax.nn.sigmoid(_dot(hb, wgate_ref[:, c:c + COLS]))
    gate_s[1, :, c:c + COLS] = jax.nn.sigmoid(_dot(hb, wgate_ref[:, D_MODEL + c:D_MODEL + c + COLS]))


def _conv_chunk(c, h_s, boin_s, u3, rows, co_ref, wconv_ref, convw_ref, next_left_ref=None):
    groups = u3.shape[0]
    total = groups * rows
    cs = slice(c, c + COLS)
    hb = h_s[...]
    ch = _dot(hb, wconv_ref[:, c:c + COLS])
    cc = _dot(hb, wconv_ref[:, 2 * D_MODEL + c:2 * D_MODEL + c + COLS])
    u = cc * ch
    u3[:, SUBLANES:SUBLANES + rows, cs] = u.reshape(groups, rows, COLS)
    um2 = u3[:, SUBLANES - 2:SUBLANES - 2 + rows, cs].reshape(total, COLS)
    um1 = u3[:, SUBLANES - 1:SUBLANES - 1 + rows, cs].reshape(total, COLS)
    conv = um2 * convw_ref[0:1, cs] + um1 * convw_ref[1:2, cs] + u * convw_ref[2:3, cs]
    cb = _dot(hb, wconv_ref[:, D_MODEL + c:D_MODEL + c + COLS])
    boin_s[:, cs] = (cb * conv).astype(BF16)
    tail = u3[:, SUBLANES + rows - CONV_LEFT:SUBLANES + rows, cs]
    co_ref[:, :, cs] = tail
    if next_left_ref is not None:
        next_left_ref[:, SUBLANES - CONV_LEFT:SUBLANES, cs] = tail


def _merge(a_s, boin_s, gate_s, mix_s, x_ref, xo_ref, wao_ref, wco_ref, wmo_ref):
    for c in range(0, D_MODEL, COLS):
        cs = slice(c, c + COLS)
        a = _dot(a_s[...], wao_ref[:, cs])
        bo = _dot(boin_s[...], wco_ref[:, cs])
        mix_s[:, cs] = (gate_s[0, :, cs] * a + gate_s[1, :, cs] * bo).astype(BF16)
    for c in range(0, D_MODEL, COLS):
        cs = slice(c, c + COLS)
        xo_ref[:, cs] = x_ref[:, cs] + _dot(mix_s[...], wmo_ref[:, cs])


BAND_HEAD_ORDER = (0, 2, 1, 3)
BAND_KEYS = WINDOW + CHUNK
START_VARIANTS = WIN_CHUNKS + 1
BLOCKS_PER_STAGE = 2
LOG2E = 1.4426950408889634
BAND_Q_SCALE = ATTN_SCALE * LOG2E


def _build_band_bias(bias_ref, slopes_ref):
    ji = lax.broadcasted_iota(jnp.int32, (BAND_KEYS, 2 * LANES), 0)
    li = lax.broadcasted_iota(jnp.int32, (BAND_KEYS, 2 * LANES), 1)
    dist = jnp.abs(li % CHUNK + WINDOW - ji).astype(F32)
    slot = li // CHUNK
    for kh in range(N_KV_HEADS):
        slope = jnp.zeros((BAND_KEYS, 2 * LANES), F32)
        for s, g in enumerate(BAND_HEAD_ORDER):
            slope = jnp.where(slot == s, slopes_ref[kh * GROUP + g], slope)
        base = -slope * dist * LOG2E
        rows = slice(kh * BAND_KEYS, (kh + 1) * BAND_KEYS)
        for v in range(START_VARIANTS):
            bias_ref[v, rows, :] = jnp.where(ji >= (WIN_CHUNKS - v) * CHUNK, base, MASKED)


def _band_query_rows(qa, qb):
    lo = _low_half(qa.shape)
    zero = jnp.zeros_like(qa)
    blocks = (
        jnp.concatenate([jnp.where(lo, qa, zero), zero], axis=1),
        jnp.concatenate([zero, jnp.where(lo, qb, zero)], axis=1),
        jnp.concatenate([jnp.where(lo, zero, qa), zero], axis=1),
        jnp.concatenate([zero, jnp.where(lo, zero, qb)], axis=1),
    )
    return jnp.concatenate(blocks, axis=0)


def _band_scores(q_s, r0, kv_head, kblock, bias):
    c0 = kv_head * GROUP * HEAD_DIM
    qs = _band_query_rows(q_s[r0:r0 + CHUNK, c0:c0 + LANES], q_s[r0:r0 + CHUNK, c0 + LANES:c0 + 2 * LANES])
    return _dot_nt(kblock, qs) + bias


def _band_softmax(st, kv_head, sink_of):
    lo = _low_half((1, LANES))
    probs, inv_den = [], []
    for col in range(2):
        s = st[:, col * LANES:(col + 1) * LANES]
        ha, hb = BAND_HEAD_ORDER[2 * col], BAND_HEAD_ORDER[2 * col + 1]
        sink = jnp.where(lo, sink_of(kv_head * GROUP + ha), sink_of(kv_head * GROUP + hb)) * LOG2E
        m = jnp.maximum(jnp.max(s, axis=0, keepdims=True), sink)
        p = jnp.exp2(s - m)
        inv_den.append(1.0 / (jnp.sum(p, axis=0, keepdims=True) + jnp.exp2(sink - m)))
        probs.append(p.astype(BF16))
    return jnp.concatenate(probs, axis=1), inv_den


def _band_values(vt_block, lead, pt):
    parts = [pt]
    if lead:
        parts.insert(0, jnp.zeros((lead, 2 * LANES), BF16))
    if vt_block.shape[1] > lead + BAND_KEYS:
        parts.append(jnp.zeros((vt_block.shape[1] - lead - BAND_KEYS, 2 * LANES), BF16))
    return _dot(vt_block, jnp.concatenate(parts, axis=0))


def _band_store(a_s, r0, kv_head, ot, inv_den):
    c0 = kv_head * GROUP * HEAD_DIM
    stacked = jnp.concatenate([ot[:, 0:LANES] * inv_den[0], ot[:, LANES:] * inv_den[1]], axis=0)
    out = stacked.T.astype(BF16)
    a_s[r0:r0 + CHUNK, c0:c0 + LANES] = out[0:CHUNK]
    a_s[r0:r0 + CHUNK, c0 + LANES:c0 + 2 * LANES] = out[CHUNK:]


def _mixer_prompt_kernel(slopes_ref, sink_ref, x_ref, g_ref, wqkv_ref, wconv_ref, wgate_ref, convw_ref,
                         wao_ref, wco_ref, wmo_ref,
                         xo_ref, ko_ref, vo_ref, co_ref,
                         h_s, q_s, k4_s, vt_s, a_s, u_s, boin_s, gate_s, mix_s, bias_s,
                         kprev_s, vtprev_s, uprev_s, *, layer, tile):
    t = pl.program_id(1)
    cur = t % 2
    nxt = 1 - cur

    @pl.when(t == 0)
    def _():
        _build_band_bias(bias_s, slopes_ref)
        kprev_s[cur] = jnp.zeros(kprev_s.shape[1:], BF16)
        vtprev_s[cur] = jnp.zeros(vtprev_s.shape[1:], BF16)
        uprev_s[cur] = jnp.zeros(uprev_s.shape[1:], F32)

    k4_s[:, 0:WINDOW, :] = kprev_s[cur]
    vt_s[:, :, 0:WINDOW] = vtprev_s[cur]
    left = slice(SUBLANES - CONV_LEFT, SUBLANES)
    u_s[:, left, :] = uprev_s[cur, :, left, :]

    h_s[...] = _rms(x_ref[...], g_ref[...]).astype(BF16)
    k, v = _project_kv(h_s, wqkv_ref)
    ko_ref[...] = k[tile - WINDOW:, :]
    vo_ref[...] = v[tile - WINDOW:, :]

    def store_new(dst_ref, head, dup):
        dup = dup.astype(BF16)
        dst_ref[head, WINDOW:WINDOW + tile, 0:LANES] = dup
        dst_ref[head, WINDOW:WINDOW + tile, LANES:] = dup

    _store_lane_tiled(k4_s, k, store_new)
    vt = v.T.astype(BF16)
    for kh in range(N_KV_HEADS):
        vt_s[kh, :, WINDOW:WINDOW + tile] = vt[kh * HEAD_DIM:(kh + 1) * HEAD_DIM, :]

    cols = range(0, D_MODEL, COLS)
    side = [(COLS, functools.partial(_q_chunk, c, h_s, q_s, wqkv_ref, BAND_Q_SCALE)) for c in cols[1:]]
    side += [(3 * COLS, functools.partial(_conv_chunk, c, h_s, boin_s, u_s, tile, co_ref, wconv_ref, convw_ref,
                                          uprev_s.at[nxt])) for c in cols]
    side += [(2 * COLS, functools.partial(_gate_chunk, c, h_s, gate_s, wgate_ref)) for c in cols]
    blocks = [(kh, r0) for kh in range(N_KV_HEADS) for r0 in range(0, tile, CHUNK)]
    stages = [blocks[i:i + BLOCKS_PER_STAGE] for i in range(0, len(blocks), BLOCKS_PER_STAGE)]
    side_per_stage = sum(cost for cost, _ in side) / len(stages)

    sink_of = lambda h: sink_ref[layer, h]

    def scores(kh, r0):
        chunk = r0 // CHUNK
        variant = jnp.where(t == 0, chunk, WIN_CHUNKS) if chunk < WIN_CHUNKS else WIN_CHUNKS
        bias = bias_s[variant, kh * BAND_KEYS:(kh + 1) * BAND_KEYS, :]
        return _band_scores(q_s, r0, kh, k4_s[kh, r0:r0 + BAND_KEYS, :], bias)

    def values(kh, r0, pt):
        lead = r0 % LANES
        return _band_values(vt_s[kh, :, r0 - lead:r0 - lead + 2 * LANES], lead, pt)

    _q_chunk(0, h_s, q_s, wqkv_ref, BAND_Q_SCALE)
    emitted = 0
    st_next = [scores(*b) for b in stages[0]]
    for i, stage in enumerate(stages):
        st = st_next
        while side and emitted < (i + 1) * side_per_stage and side[0][0] == COLS:
            cost, emit = side.pop(0)
            emit()
            emitted += cost
        if i + 1 < len(stages):
            st_next = [scores(*b) for b in stages[i + 1]]
        soft = [_band_softmax(s, kh, sink_of) for s, (kh, _) in zip(st, stage)]
        while side and emitted < (i + 1) * side_per_stage:
            cost, emit = side.pop(0)
            emit()
            emitted += cost
        outs = [values(kh, r0, pt) for (pt, _), (kh, r0) in zip(soft, stage)]
        for ot, (_, inv_den), (kh, r0) in zip(outs, soft, stage):
            _band_store(a_s, r0, kh, ot, inv_den)
    for _, emit in side:
        emit()

    kprev_s[nxt] = k4_s[:, tile:tile + WINDOW, :]
    vtprev_s[nxt] = vt_s[:, :, tile:tile + WINDOW]

    _merge(a_s, boin_s, gate_s, mix_s, x_ref, xo_ref, wao_ref, wco_ref, wmo_ref)


def _mixer_sample_kernel(slopes_ref, sink_ref, x_ref, g_ref, kc_ref, vc_ref, st_ref,
                         wqkv_ref, wconv_ref, wgate_ref, convw_ref, wao_ref, wco_ref, wmo_ref,
                         xo_ref, ko_ref, vo_ref, co_ref,
                         h_s, q_s, k4_s, v4_s, a_s, u_s, boin_s, gate_s, mix_s, bias_s, *, layer, seqs, rows):
    keys = WINDOW + rows

    @pl.when(pl.program_id(0) == 0)
    def _():
        _build_score_bias(bias_s, slopes_ref, sink_ref, layer, rows, keys, (0,))

    u_s[:, SUBLANES - CONV_LEFT:SUBLANES, :] = st_ref[...]

    h_s[...] = _rms(x_ref[...], g_ref[...]).astype(BF16)
    for c in range(0, D_MODEL, COLS):
        _q_chunk(c, h_s, q_s, wqkv_ref, ATTN_SCALE)
    k, v = _project_kv(h_s, wqkv_ref)
    ko_ref[...] = k
    vo_ref[...] = v

    def store_cache(dst_ref, head, dup):
        dup = dup.reshape(seqs, WINDOW, LANES).astype(BF16)
        dst_ref[head, :, 0:WINDOW, 0:LANES] = dup
        dst_ref[head, :, 0:WINDOW, LANES:] = dup

    def store_new(dst_ref, head, dup):
        dup = dup.reshape(seqs, rows, LANES).astype(BF16)
        dst_ref[head, :, WINDOW:keys, 0:LANES] = dup
        dst_ref[head, :, WINDOW:keys, LANES:] = dup

    _store_lane_tiled(k4_s, kc_ref[...].reshape(seqs * WINDOW, KV_DIM), store_cache)
    _store_lane_tiled(v4_s, vc_ref[...].reshape(seqs * WINDOW, KV_DIM), store_cache)
    _store_lane_tiled(k4_s, k, store_new)
    _store_lane_tiled(v4_s, v, store_new)

    units = [
        _group_attention_unit(
            q_s, a_s, g * rows, rows, kh,
            functools.partial(lambda kh, g: k4_s[kh, g], kh, g),
            functools.partial(lambda kh, g: v4_s[kh, g], kh, g),
            functools.partial(lambda kh: bias_s[0, kh * GROUP * rows:(kh + 1) * GROUP * rows, :], kh))
        for kh in range(N_KV_HEADS) for g in range(seqs)]
    side = [(3 * COLS, functools.partial(_conv_chunk, c, h_s, boin_s, u_s, rows, co_ref, wconv_ref, convw_ref))
            for c in range(0, D_MODEL, COLS)]
    side += [(2 * COLS, functools.partial(_gate_chunk, c, h_s, gate_s, wgate_ref)) for c in range(0, D_MODEL, COLS)]
    _emit_pipelined(units, side)
    _merge(a_s, boin_s, gate_s, mix_s, x_ref, xo_ref, wao_ref, wco_ref, wmo_ref)


def _cross_attention_units(cq_s, co_s, mk_ref, mv_ref, groups, rows):
    lo = _low_half((rows, LANES))
    zero = jnp.zeros((rows, LANES), BF16)
    stacked = MEM_HEADS * rows <= 2 * LANES
    units = []
    for g in range(groups):
        rs = slice(g * rows, (g + 1) * rows)
        if stacked:
            def scores(rs=rs, g=g):
                return _dot_nt(_head_rows(cq_s[rs, 0:LANES], cq_s[rs, LANES:]), mk_ref[g].astype(BF16))

            def values(p, rs=rs, g=g):
                oa, ob = _pick_heads(_dot(p.astype(BF16), mv_ref[g].astype(BF16)), rows)
                co_s[rs, 0:LANES] = oa.astype(BF16)
                co_s[rs, LANES:] = ob.astype(BF16)

            units.append((scores, _softmax, values))
            continue
        held = {}
        for hh in range(MEM_HEADS):
            c, odd = divmod(hh, 2)

            def scores(rs=rs, g=g, c=c, odd=odd):
                qcol = cq_s[rs, c * LANES:(c + 1) * LANES]
                qm = jnp.where(lo, zero, qcol) if odd else jnp.where(lo, qcol, zero)
                qm = jnp.concatenate([qm, zero] if c == 0 else [zero, qm], axis=1)
                return _dot_nt(qm, mk_ref[g].astype(BF16))

            def values(p, rs=rs, g=g, c=c, odd=odd, held=held):
                out = _dot(p.astype(BF16), mv_ref[g].astype(BF16))[:, c * LANES:(c + 1) * LANES]
                if odd:
                    co_s[rs, c * LANES:(c + 1) * LANES] = jnp.where(lo, held.pop(c), out).astype(BF16)
                else:
                    held[c] = out

            units.append((scores, _softmax, values))
    return units


def _ffn_kernel(x_ref, mk_ref, mv_ref, gc_ref, wcq_ref, wcout_ref, gm_ref, wup_ref, wdown_ref, gf_ref,
                xo_ref,
                h_s, cq_s, co_s, x1_s, hm_s, act_s, *, groups, rows, final):
    i = pl.program_id(0)
    cur = i % 2
    nxt = 1 - cur

    @pl.when(i == 0)
    def _():
        x1_s[cur] = jnp.zeros(x1_s.shape[1:], F32)
        hm_s[cur] = jnp.zeros(hm_s.shape[1:], BF16)

    def norm_in():
        h_s[...] = _rms(x_ref[...], gc_ref[...]).astype(BF16)

    def project_q():
        cq_s[...] = (_dot(h_s[...], wcq_ref[...]) * MEM_SCALE).astype(BF16)

    def residual(c):
        cs = slice(c, c + COLS)
        x1_s[nxt, :, cs] = x_ref[:, cs] + _dot(co_s[...], wcout_ref[:, cs])

    def norm_mlp():
        hm_s[nxt] = _rms(x1_s[nxt], gm_ref[...]).astype(BF16)

    units = _cross_attention_units(cq_s, co_s, mk_ref, mv_ref, groups, rows)
    pieces = [norm_in, project_q]
    state = {}
    for j in range(len(units) + 2):
        def piece(j=j):
            if j < len(units):
                state[j] = units[j][0]()
            if 0 <= j - 1 < len(units):
                state[j - 1] = units[j - 1][1](state[j - 1])
            if 0 <= j - 2 < len(units):
                units[j - 2][2](state.pop(j - 2))
        pieces.append(piece)
    half = D_MODEL // 2
    pieces += [lambda: [residual(c) for c in range(0, half, COLS)],
               lambda: [residual(c) for c in range(half, D_MODEL, COLS)],
               norm_mlp]

    for c in range(0, D_FF, COLS):
        up = jnp.maximum(_dot(hm_s[cur], wup_ref[:, c:c + COLS]), 0.0)
        act_s[:, c:c + COLS] = (up * up).astype(BF16)
        if pieces:
            pieces.pop(0)()
    for piece in pieces:
        piece()
    for c in range(0, D_MODEL, COLS):
        cs = slice(c, c + COLS)
        x1_s[cur, :, cs] = x1_s[cur, :, cs] + _dot(act_s[...], wdown_ref[:, cs])
    if final:
        xo_ref[...] = _rms(x1_s[cur], gf_ref[...])
    else:
        xo_ref[...] = x1_s[cur]


def _memkv_kernel(mem_ref, g_ref, w_ref, mk_ref, mv_ref):
    h = _rms(mem_ref[...], g_ref[...]).astype(BF16)
    kv = _dot(h, w_ref[...])
    mk_ref[...] = kv[:, 0:MEM_DIM]
    mv_ref[...] = kv[:, MEM_DIM:]


def _params(n_axes):
    return pltpu.CompilerParams(dimension_semantics=("arbitrary",) * n_axes, vmem_limit_bytes=VMEM_LIMIT)


def _resident(shape, layer):
    zeros = (0,) * len(shape)
    return pl.BlockSpec((None,) + tuple(shape), lambda *_: (layer,) + zeros, pipeline_mode=pl.Buffered(1))


def _smem():
    return pl.BlockSpec(memory_space=pltpu.SMEM)


def _mixer_weight_specs(layer):
    return [
        _resident((D_MODEL, D_MODEL + 2 * KV_DIM), layer),
        _resident((D_MODEL, 3 * D_MODEL), layer),
        _resident((D_MODEL, 2 * D_MODEL), layer),
        _resident((CONV_TAPS, D_MODEL), layer),
        _resident((D_MODEL, D_MODEL), layer),
        _resident((D_MODEL, D_MODEL), layer),
        _resident((D_MODEL, D_MODEL), layer),
    ]


def _mixer_weights(w):
    return (w["w_qkv"], w["w_conv"], w["w_gate"], w["conv_w"], w["w_attn_out"], w["w_conv_out"], w["w_mix_out"])


def _mixer_prompt(layer, x, slopes, w):
    batch, seq, _ = x.shape
    tile = PROMPT_TILE
    row = lambda b, t: (b, t, 0)
    per_seq = lambda b, t: (b, 0, 0)
    return pl.pallas_call(
        functools.partial(_mixer_prompt_kernel, layer=layer, tile=tile),
        grid=(batch, seq // tile),
        in_specs=[_smem(), _smem(),
                  pl.BlockSpec((None, tile, D_MODEL), row),
                  _resident((1, D_MODEL), layer)] + _mixer_weight_specs(layer),
        out_specs=[pl.BlockSpec((None, tile, D_MODEL), row),
                   pl.BlockSpec((None, WINDOW, KV_DIM), per_seq),
                   pl.BlockSpec((None, WINDOW, KV_DIM), per_seq),
                   pl.BlockSpec((1, CONV_LEFT, D_MODEL), per_seq)],
        out_shape=[jax.ShapeDtypeStruct(x.shape, F32),
                   jax.ShapeDtypeStruct((batch, WINDOW, KV_DIM), F32),
                   jax.ShapeDtypeStruct((batch, WINDOW, KV_DIM), F32),
                   jax.ShapeDtypeStruct((batch, CONV_LEFT, D_MODEL), F32)],
        scratch_shapes=[
            pltpu.VMEM((tile, D_MODEL), BF16),
            pltpu.VMEM((tile, D_MODEL), BF16),
            pltpu.VMEM((N_KV_HEADS, WINDOW + tile, 2 * LANES), BF16),
            pltpu.VMEM((N_KV_HEADS, HEAD_DIM, WINDOW + tile), BF16),
            pltpu.VMEM((tile, D_MODEL), BF16),
            pltpu.VMEM((1, SUBLANES + tile, D_MODEL), F32),
            pltpu.VMEM((tile, D_MODEL), BF16),
            pltpu.VMEM((2, tile, D_MODEL), F32),
            pltpu.VMEM((tile, D_MODEL), BF16),
            pltpu.VMEM((START_VARIANTS, N_KV_HEADS * BAND_KEYS, 2 * LANES), F32),
            pltpu.VMEM((2, N_KV_HEADS, WINDOW, 2 * LANES), BF16),
            pltpu.VMEM((2, N_KV_HEADS, HEAD_DIM, WINDOW), BF16),
            pltpu.VMEM((2, 1, SUBLANES, D_MODEL), F32),
        ],
        compiler_params=_params(2),
        name=f"mixer_prompt_l{layer}",
    )(slopes, w["sink"], x, w["g_mix"], *_mixer_weights(w))


def _mixer_sample(layer, x2, slopes, kc, vc, st, w, rows):
    total = x2.shape[0]
    seqs = SAMPLE_SEQS
    tile = seqs * rows
    keys = WINDOW + rows
    row = lambda i: (i, 0)
    per_seq = lambda i: (layer, i, 0, 0)
    return pl.pallas_call(
        functools.partial(_mixer_sample_kernel, layer=layer, seqs=seqs, rows=rows),
        grid=(total // tile,),
        in_specs=[_smem(), _smem(),
                  pl.BlockSpec((tile, D_MODEL), row),
                  _resident((1, D_MODEL), layer),
                  pl.BlockSpec((None, seqs, WINDOW, KV_DIM), per_seq),
                  pl.BlockSpec((None, seqs, WINDOW, KV_DIM), per_seq),
                  pl.BlockSpec((None, seqs, CONV_LEFT, D_MODEL), per_seq),
                  ] + _mixer_weight_specs(layer),
        out_specs=[pl.BlockSpec((tile, D_MODEL), row),
                   pl.BlockSpec((tile, KV_DIM), row),
                   pl.BlockSpec((tile, KV_DIM), row),
                   pl.BlockSpec((seqs, CONV_LEFT, D_MODEL), lambda i: (i, 0, 0))],
        out_shape=[jax.ShapeDtypeStruct(x2.shape, F32),
                   jax.ShapeDtypeStruct((total, KV_DIM), F32),
                   jax.ShapeDtypeStruct((total, KV_DIM), F32),
                   jax.ShapeDtypeStruct((total // rows, CONV_LEFT, D_MODEL), F32)],
        scratch_shapes=[
            pltpu.VMEM((tile, D_MODEL), BF16),
            pltpu.VMEM((tile, D_MODEL), BF16),
            pltpu.VMEM((N_KV_HEADS, seqs, keys, 2 * LANES), BF16),
            pltpu.VMEM((N_KV_HEADS, seqs, keys, 2 * LANES), BF16),
            pltpu.VMEM((tile, D_MODEL), BF16),
            pltpu.VMEM((seqs, SUBLANES + rows, D_MODEL), F32),
            pltpu.VMEM((tile, D_MODEL), BF16),
            pltpu.VMEM((2, tile, D_MODEL), F32),
            pltpu.VMEM((tile, D_MODEL), BF16),
            pltpu.VMEM((1, N_HEADS * rows, SCORE_COLS), F32),
        ],
        compiler_params=_params(1),
        name=f"mixer_sample_l{layer}",
    )(slopes, w["sink"], x2, w["g_mix"], kc, vc, st, *_mixer_weights(w))


def _ffn(layer, x2, mk, mv, w, groups, rows, final, name):
    total = x2.shape[0]
    tile = groups * rows
    steps_per_group = (total // mk.shape[1]) // rows if groups == 1 else 1
    n_tiles = total // tile
    tile_in = lambda i: jnp.minimum(i, n_tiles - 1)
    mem_spec = pl.BlockSpec((None, groups, N_MEM, MEM_DIM),
                            lambda i: (layer, tile_in(i) // steps_per_group, 0, 0))
    return pl.pallas_call(
        functools.partial(_ffn_kernel, groups=groups, rows=rows, final=final),
        grid=(n_tiles + 1,),
        in_specs=[pl.BlockSpec((tile, D_MODEL), lambda i: (tile_in(i), 0)), mem_spec, mem_spec,
                  _resident((1, D_MODEL), layer),
                  _resident((D_MODEL, MEM_DIM), layer),
                  _resident((MEM_DIM, D_MODEL), layer),
                  _resident((1, D_MODEL), layer),
                  _resident((D_MODEL, D_FF), layer),
                  _resident((D_FF, D_MODEL), layer),
                  pl.BlockSpec((1, D_MODEL), lambda i: (0, 0))],
        out_specs=pl.BlockSpec((tile, D_MODEL), lambda i: (jnp.maximum(i - 1, 0), 0)),
        out_shape=jax.ShapeDtypeStruct(x2.shape, F32),
        scratch_shapes=[
            pltpu.VMEM((tile, D_MODEL), BF16),
            pltpu.VMEM((tile, MEM_DIM), BF16),
            pltpu.VMEM((tile, MEM_DIM), BF16),
            pltpu.VMEM((2, tile, D_MODEL), F32),
            pltpu.VMEM((2, tile, D_MODEL), BF16),
            pltpu.VMEM((tile, D_FF), BF16),
        ],
        compiler_params=_params(1),
        name=name,
    )(x2, mk, mv, w["g_cross"], w["w_cq"], w["w_co"], w["g_mlp"], w["w_up"], w["w_down"], w["g_final"])


def _memory_kv(mem, g_mem, w_ckv):
    batch = mem.shape[0]
    out = jax.ShapeDtypeStruct((DEPTH, batch, N_MEM, MEM_DIM), F32)
    return pl.pallas_call(
        _memkv_kernel,
        grid=(DEPTH, batch),
        in_specs=[pl.BlockSpec((None, N_MEM, D_MODEL), lambda l, b: (b, 0, 0)),
                  pl.BlockSpec((None, 1, D_MODEL), lambda l, b: (l, 0, 0)),
                  pl.BlockSpec((None, D_MODEL, 2 * MEM_DIM), lambda l, b: (l, 0, 0))],
        out_specs=[pl.BlockSpec((None, None, N_MEM, MEM_DIM), lambda l, b: (l, b, 0, 0))] * 2,
        out_shape=[out, out],
        compiler_params=_params(2),
        name="memory_kv",
    )(mem, g_mem, w_ckv)


def kernel(x_prompt, x_sample, mem_prompt, cache_attn_k, cache_attn_v, state_conv, cache_mem_k, cache_mem_v,
           norm_mix_g, w_in, conv_w, attn_sink, w_attn_out, w_conv_out, w_mix_out, norm_cross_g, norm_mem_g,
           w_cq, w_ckv, w_co, norm_mlp_g, w_up, w_down, norm_final_g):
    batch, seq, _ = x_prompt.shape
    dec_batch, dec_seq, _ = x_sample.shape
    assert seq % PROMPT_TILE == 0 and PROMPT_TILE % LANES == 0 and PROMPT_TILE >= WINDOW
    assert dec_seq <= CHUNK and dec_seq % 16 == 0 and dec_batch % SAMPLE_SEQS == 0
    assert cache_attn_k.shape[2] == WINDOW

    n_q = N_HEADS * HEAD_DIM
    n_conv = n_q + 2 * KV_DIM
    n_gate = n_conv + 3 * D_MODEL
    w = {
        "sink": attn_sink,
        "g_mix": norm_mix_g[:, None, :],
        "w_qkv": w_in[:, :, :n_conv].astype(BF16),
        "w_conv": w_in[:, :, n_conv:n_gate].astype(BF16),
        "w_gate": w_in[:, :, n_gate:].astype(BF16),
        "conv_w": conv_w,
        "w_attn_out": w_attn_out.astype(BF16),
        "w_conv_out": w_conv_out.astype(BF16),
        "w_mix_out": w_mix_out.astype(BF16),
        "g_cross": norm_cross_g[:, None, :],
        "w_cq": w_cq.astype(BF16),
        "w_co": w_co.astype(BF16),
        "g_mlp": norm_mlp_g[:, None, :],
        "w_up": w_up.astype(BF16),
        "w_down": w_down.astype(BF16),
        "g_final": norm_final_g[None, :],
    }
    slopes = jnp.exp2(-8.0 * jnp.arange(1, N_HEADS + 1, dtype=F32) / N_HEADS)

    mk_p, mv_p = _memory_kv(mem_prompt, norm_mem_g[:, None, :], w_ckv.astype(BF16))

    kc = cache_attn_k.reshape(DEPTH, dec_batch, WINDOW, KV_DIM)
    vc = cache_attn_v.reshape(DEPTH, dec_batch, WINDOW, KV_DIM)
    mk_s = cache_mem_k.reshape(DEPTH, dec_batch, N_MEM, MEM_DIM)
    mv_s = cache_mem_v.reshape(DEPTH, dec_batch, N_MEM, MEM_DIM)

    xp = x_prompt
    xs = x_sample.reshape(dec_batch * dec_seq, D_MODEL)
    kp_l, vp_l, cp_l, ks_l, vs_l, cs_l = [], [], [], [], [], []
    for layer in range(DEPTH):
        final = layer == DEPTH - 1
        xp, kp, vp, cp = _mixer_prompt(layer, xp, slopes, w)
        xp = _ffn(layer, xp.reshape(batch * seq, D_MODEL), mk_p, mv_p, w, 1, PROMPT_TILE, final,
                  f"ffn_prompt_l{layer}").reshape(batch, seq, D_MODEL)
        xs, kn, vn, cs = _mixer_sample(layer, xs, slopes, kc, vc, state_conv, w, dec_seq)
        xs = _ffn(layer, xs, mk_s, mv_s, w, SAMPLE_SEQS, dec_seq, final, f"ffn_sample_l{layer}")
        kp_l.append(kp)
        vp_l.append(vp)
        cp_l.append(cp)
        ks_l.append(kn)
        vs_l.append(vn)
        cs_l.append(cs)

    kv_p = (DEPTH, batch, WINDOW, N_KV_HEADS, HEAD_DIM)
    kv_s = (DEPTH, dec_batch, dec_seq, N_KV_HEADS, HEAD_DIM)
    mem_shape = (DEPTH, batch, N_MEM, MEM_HEADS, MEM_DIM // MEM_HEADS)
    return (xp, xs.reshape(dec_batch, dec_seq, D_MODEL),
            jnp.stack(kp_l).reshape(kv_p), jnp.stack(vp_l).reshape(kv_p), jnp.stack(cp_l),
            mk_p.reshape(mem_shape), mv_p.reshape(mem_shape),
            jnp.stack(ks_l).reshape(kv_s), jnp.stack(vs_l).reshape(kv_s), jnp.stack(cs_l))
```

```python
import functools

import jax
import jax.numpy as jnp
from jax import lax
from jax.experimental import pallas as pl
from jax.experimental.pallas import tpu as pltpu

D_MODEL = 1024
DEPTH = 4
CHUNK = 64
WINDOW = 128
WIN_CHUNKS = WINDOW // CHUNK
HEAD_DIM = 64
N_HEADS = 16
N_KV_HEADS = 4
GROUP = N_HEADS // N_KV_HEADS
KV_DIM = N_KV_HEADS * HEAD_DIM
N_MEM = 256
MEM_HEADS = 4
MEM_DIM = 256
D_FF = 4 * D_MODEL
IN_CONV = N_HEADS * HEAD_DIM + 2 * KV_DIM
IN_GATE = IN_CONV + 3 * D_MODEL
IN_COLS = IN_GATE + 2 * D_MODEL
EPS = 1e-6
ATTN_SCALE = HEAD_DIM ** -0.5
MEM_SCALE = (MEM_DIM // MEM_HEADS) ** -0.5
CONV_TAPS = 3
CONV_LEFT = CONV_TAPS - 1

LANES = 128
SUBLANES = 8
HALF = LANES // 2
SCORE_COLS = 2 * LANES
MASKED = -1e30
V7X_VMEM_BYTES = 64 * 1024 * 1024
VMEM_LIMIT = V7X_VMEM_BYTES - 8 * 1024 * 1024

PROMPT_TILE = 512
FFN_TILE = 512
SAMPLE_SEQS = 8
COLS = 256

BF16 = jnp.bfloat16
F32 = jnp.float32


def _dot(a, b):
    return jnp.dot(a, b, preferred_element_type=F32)


def _dot_nt(a, b):
    return lax.dot_general(a, b, (((1,), (1,)), ((), ())), preferred_element_type=F32)


def _rms(x, g):
    return x * lax.rsqrt(jnp.mean(x * x, axis=-1, keepdims=True) + EPS) * g


def _low_half(shape):
    return lax.broadcasted_iota(jnp.int32, shape, len(shape) - 1) < HALF


def _dup_halves(col):
    swapped = pltpu.roll(col, HALF, axis=1)
    lo = _low_half(col.shape)
    return jnp.where(lo, col, swapped), jnp.where(lo, swapped, col)


def _store_lane_tiled(dst_ref, kv, store):
    for c in range(KV_DIM // LANES):
        even, odd = _dup_halves(kv[:, c * LANES:(c + 1) * LANES])
        store(dst_ref, 2 * c, even)
        store(dst_ref, 2 * c + 1, odd)


def _pad_keys(block):
    pad = jnp.zeros((SCORE_COLS - block.shape[0], block.shape[1]), block.dtype)
    return jnp.concatenate([block, pad], axis=0)


def _head_rows(qa, qb):
    lo = _low_half(qa.shape)
    zero = jnp.zeros_like(qa)
    blocks = (
        jnp.concatenate([jnp.where(lo, qa, zero), zero], axis=1),
        jnp.concatenate([jnp.where(lo, zero, qa), zero], axis=1),
        jnp.concatenate([zero, jnp.where(lo, qb, zero)], axis=1),
        jnp.concatenate([zero, jnp.where(lo, zero, qb)], axis=1),
    )
    return jnp.concatenate(blocks, axis=0)


def _pick_heads(o4, rows):
    lo = _low_half((rows, LANES))
    oa = jnp.where(lo, o4[0:rows, 0:LANES], o4[rows:2 * rows, 0:LANES])
    ob = jnp.where(lo, o4[2 * rows:3 * rows, LANES:], o4[3 * rows:4 * rows, LANES:])
    return oa, ob


def _build_score_bias(bias_ref, slopes_ref, sink_ref, layer, nq, nk, first_valid_keys):
    qi = lax.broadcasted_iota(jnp.int32, (nq, SCORE_COLS), 0)
    ji = lax.broadcasted_iota(jnp.int32, (nq, SCORE_COLS), 1)
    dist = jnp.abs(qi + WINDOW - ji).astype(F32)
    for h in range(N_HEADS):
        base = jnp.where(ji == nk, sink_ref[layer, h], -slopes_ref[h] * dist)
        for v, first in enumerate(first_valid_keys):
            valid = (ji >= first) & (ji <= nk)
            bias_ref[v, h * nq:(h + 1) * nq, :] = jnp.where(valid, base, MASKED)


def _softmax(s):
    p = jnp.exp(s - jnp.max(s, axis=-1, keepdims=True))
    return p * (1.0 / jnp.sum(p, axis=-1, keepdims=True))


def _group_attention_unit(q_s, a_s, r0, nq, kv_head, kblock_of, vblock_of, bias_of):
    c0 = kv_head * GROUP * HEAD_DIM

    def scores():
        qs = _head_rows(q_s[r0:r0 + nq, c0:c0 + LANES], q_s[r0:r0 + nq, c0 + LANES:c0 + 2 * LANES])
        return _dot_nt(qs, _pad_keys(kblock_of())) + bias_of()

    def values(p):
        oa, ob = _pick_heads(_dot(p.astype(BF16), _pad_keys(vblock_of())), nq)
        a_s[r0:r0 + nq, c0:c0 + LANES] = oa.astype(BF16)
        a_s[r0:r0 + nq, c0 + LANES:c0 + 2 * LANES] = ob.astype(BF16)

    return scores, _softmax, values


def _emit_pipelined(units, side):
    slots = len(units) + 2
    total = sum(cost for cost, _ in side)
    emitted = 0
    state = {}
    for j in range(slots):
        if j < len(units):
            state[j] = units[j][0]()
        if 0 <= j - 1 < len(units):
            state[j - 1] = units[j - 1][1](state[j - 1])
        while side and emitted * slots < (j + 1) * total:
            cost, emit = side.pop(0)
            emit()
            emitted += cost
        if 0 <= j - 2 < len(units):
            units[j - 2][2](state.pop(j - 2))
    for _, emit in side:
        emit()


def _q_chunk(c, h_s, q_s, wqkv_ref, scale):
    q_s[:, c:c + COLS] = (_dot(h_s[...], wqkv_ref[:, c:c + COLS]) * scale).astype(BF16)


def _project_kv(h_s, wqkv_ref):
    hb = h_s[...]
    k = _dot(hb, wqkv_ref[:, D_MODEL:D_MODEL + KV_DIM])
    v = _dot(hb, wqkv_ref[:, D_MODEL + KV_DIM:D_MODEL + 2 * KV_DIM])
    return k, v


def _gate_chunk(c, h_s, gate_s, wgate_ref):
    hb = h_s[...]
    gate_s[0, :, c:c + COLS] = jax.nn.sigmoid(_dot(hb, wgate_ref[:, c:c + COLS]))
    gate_s[1, :, c:c + COLS] = jax.nn.sigmoid(_dot(hb, wgate_ref[:, D_MODEL + c:D_MODEL + c + COLS]))


def _conv_chunk(c, h_s, boin_s, u3, rows, co_ref, wconv_ref, convw_ref, next_left_ref=None):
    groups = u3.shape[0]
    total = groups * rows
    cs = slice(c, c + COLS)
    hb = h_s[...]
    ch = _dot(hb, wconv_ref[:, c:c + COLS])
    cc = _dot(hb, wconv_ref[:, 2 * D_MODEL + c:2 * D_MODEL + c + COLS])
    u = cc * ch
    u3[:, SUBLANES:SUBLANES + rows, cs] = u.reshape(groups, rows, COLS)
    um2 = u3[:, SUBLANES - 2:SUBLANES - 2 + rows, cs].reshape(total, COLS)
    um1 = u3[:, SUBLANES - 1:SUBLANES - 1 + rows, cs].reshape(total, COLS)
    conv = um2 * convw_ref[0:1, cs] + um1 * convw_ref[1:2, cs] + u * convw_ref[2:3, cs]
    cb = _dot(hb, wconv_ref[:, D_MODEL + c:D_MODEL + c + COLS])
    boin_s[:, cs] = (cb * conv).astype(BF16)
    tail = u3[:, SUBLANES + rows - CONV_LEFT:SUBLANES + rows, cs]
    co_ref[:, :, cs] = tail
    if next_left_ref is not None:
        next_left_ref[:, SUBLANES - CONV_LEFT:SUBLANES, cs] = tail


def _merge(a_s, boin_s, gate_s, mix_s, x_ref, xo_ref, wao_ref, wco_ref, wmo_ref, vector_side=None):
    for c in range(0, D_MODEL, COLS):
        cs = slice(c, c + COLS)
        a = _dot(a_s[...], wao_ref[:, cs])
        bo = _dot(boin_s[...], wco_ref[:, cs])
        mix_s[:, cs] = (gate_s[0, :, cs] * a + gate_s[1, :, cs] * bo).astype(BF16)
        if c == 0 and vector_side is not None:
            vector_side()
    for c in range(0, D_MODEL, COLS):
        cs = slice(c, c + COLS)
        xo_ref[:, cs] = x_ref[:, cs] + _dot(mix_s[...], wmo_ref[:, cs])


BAND_HEAD_ORDER = (0, 2, 1, 3)
BAND_KEYS = WINDOW + CHUNK
START_VARIANTS = WIN_CHUNKS + 1
BLOCKS_PER_STAGE = 2
LOG2E = 1.4426950408889634
BAND_Q_SCALE = ATTN_SCALE * LOG2E


def _build_band_bias(bias_ref, slopes_ref):
    ji = lax.broadcasted_iota(jnp.int32, (BAND_KEYS, 2 * LANES), 0)
    li = lax.broadcasted_iota(jnp.int32, (BAND_KEYS, 2 * LANES), 1)
    dist = jnp.abs(li % CHUNK + WINDOW - ji).astype(F32)
    slot = li // CHUNK
    for kh in range(N_KV_HEADS):
        slope = jnp.zeros((BAND_KEYS, 2 * LANES), F32)
        for s, g in enumerate(BAND_HEAD_ORDER):
            slope = jnp.where(slot == s, slopes_ref[kh * GROUP + g], slope)
        base = -slope * dist * LOG2E
        rows = slice(kh * BAND_KEYS, (kh + 1) * BAND_KEYS)
        for v in range(START_VARIANTS):
            bias_ref[v, rows, :] = jnp.where(ji >= (WIN_CHUNKS - v) * CHUNK, base, MASKED)


def _band_query_rows(qa, qb):
    lo = _low_half(qa.shape)
    zero = jnp.zeros_like(qa)
    blocks = (
        jnp.concatenate([jnp.where(lo, qa, zero), zero], axis=1),
        jnp.concatenate([zero, jnp.where(lo, qb, zero)], axis=1),
        jnp.concatenate([jnp.where(lo, zero, qa), zero], axis=1),
        jnp.concatenate([zero, jnp.where(lo, zero, qb)], axis=1),
    )
    return jnp.concatenate(blocks, axis=0)


def _band_scores(q_s, r0, kv_head, kblock, bias):
    c0 = kv_head * GROUP * HEAD_DIM
    qs = _band_query_rows(q_s[r0:r0 + CHUNK, c0:c0 + LANES], q_s[r0:r0 + CHUNK, c0 + LANES:c0 + 2 * LANES])
    return _dot_nt(kblock, qs) + bias


def _band_softmax(st, kv_head, sink_of):
    lo = _low_half((1, LANES))
    probs, inv_den = [], []
    for col in range(2):
        s = st[:, col * LANES:(col + 1) * LANES]
        ha, hb = BAND_HEAD_ORDER[2 * col], BAND_HEAD_ORDER[2 * col + 1]
        sink = jnp.where(lo, sink_of(kv_head * GROUP + ha), sink_of(kv_head * GROUP + hb)) * LOG2E
        m = jnp.maximum(jnp.max(s, axis=0, keepdims=True), sink)
        p = jnp.exp2(s - m)
        inv_den.append(1.0 / (jnp.sum(p, axis=0, keepdims=True) + jnp.exp2(sink - m)))
        probs.append(p.astype(BF16))
    return jnp.concatenate(probs, axis=1), inv_den


def _band_values(vt_block, lead, pt):
    parts = [pt]
    if lead:
        parts.insert(0, jnp.zeros((lead, 2 * LANES), BF16))
    if vt_block.shape[1] > lead + BAND_KEYS:
        parts.append(jnp.zeros((vt_block.shape[1] - lead - BAND_KEYS, 2 * LANES), BF16))
    return _dot(vt_block, jnp.concatenate(parts, axis=0))


def _band_store(a_s, r0, kv_head, ot, inv_den):
    c0 = kv_head * GROUP * HEAD_DIM
    stacked = jnp.concatenate([ot[:, 0:LANES] * inv_den[0], ot[:, LANES:] * inv_den[1]], axis=0)
    out = stacked.T.astype(BF16)
    a_s[r0:r0 + CHUNK, c0:c0 + LANES] = out[0:CHUNK]
    a_s[r0:r0 + CHUNK, c0 + LANES:c0 + 2 * LANES] = out[CHUNK:]


def _mixer_prompt_kernel(slopes_ref, sink_ref, x_ref, xn_ref, g_ref, win_ref, convw_ref,
                         wao_ref, wco_ref, wmo_ref,
                         xo_ref, ko_ref, vo_ref, co_ref,
                         h2_s, q_s, k4_s, vt_s, a_s, u_s, boin_s, gate_s, mix_s, bias_s,
                         kprev_s, vtprev_s, uprev_s, *, layer, tile):
    wqkv_ref, wconv_ref, wgate_ref = _split_w_in(win_ref)
    t = pl.program_id(1)
    cur = t % 2
    nxt = 1 - cur

    @pl.when(t == 0)
    def _():
        _build_band_bias(bias_s, slopes_ref)
        kprev_s[cur] = jnp.zeros(kprev_s.shape[1:], BF16)
        vtprev_s[cur] = jnp.zeros(vtprev_s.shape[1:], BF16)
        uprev_s[cur] = jnp.zeros(uprev_s.shape[1:], F32)
        h2_s[cur] = _rms(x_ref[...], g_ref[...]).astype(BF16)

    h_s = h2_s.at[cur]

    def norm_next_tile():
        h2_s[nxt] = _rms(xn_ref[...], g_ref[...]).astype(BF16)

    k4_s[:, 0:WINDOW, :] = kprev_s[cur]
    vt_s[:, :, 0:WINDOW] = vtprev_s[cur]
    left = slice(SUBLANES - CONV_LEFT, SUBLANES)
    u_s[:, left, :] = uprev_s[cur, :, left, :]

    k, v = _project_kv(h_s, wqkv_ref)
    _q_chunk(0, h_s, q_s, wqkv_ref, BAND_Q_SCALE)
    ko_ref[...] = k[tile - WINDOW:, :]
    vo_ref[...] = v[tile - WINDOW:, :]

    def store_new(dst_ref, head, dup):
        dup = dup.astype(BF16)
        dst_ref[head, WINDOW:WINDOW + tile, 0:LANES] = dup
        dst_ref[head, WINDOW:WINDOW + tile, LANES:] = dup

    _store_lane_tiled(k4_s, k, store_new)
    vt = v.T.astype(BF16)
    for kh in range(N_KV_HEADS):
        vt_s[kh, :, WINDOW:WINDOW + tile] = vt[kh * HEAD_DIM:(kh + 1) * HEAD_DIM, :]

    cols = range(0, D_MODEL, COLS)
    side = [(COLS, functools.partial(_q_chunk, c, h_s, q_s, wqkv_ref, BAND_Q_SCALE)) for c in cols[1:]]
    side += [(3 * COLS, functools.partial(_conv_chunk, c, h_s, boin_s, u_s, tile, co_ref, wconv_ref, convw_ref,
                                          uprev_s.at[nxt])) for c in cols]
    side += [(2 * COLS, functools.partial(_gate_chunk, c, h_s, gate_s, wgate_ref)) for c in cols]
    blocks = [(kh, r0) for kh in range(N_KV_HEADS) for r0 in range(0, tile, CHUNK)]
    stages = [blocks[i:i + BLOCKS_PER_STAGE] for i in range(0, len(blocks), BLOCKS_PER_STAGE)]
    side_per_stage = sum(cost for cost, _ in side) / len(stages)

    sink_of = lambda h: sink_ref[layer, h]

    def scores(kh, r0):
        chunk = r0 // CHUNK
        variant = jnp.where(t == 0, chunk, WIN_CHUNKS) if chunk < WIN_CHUNKS else WIN_CHUNKS
        bias = bias_s[variant, kh * BAND_KEYS:(kh + 1) * BAND_KEYS, :]
        return _band_scores(q_s, r0, kh, k4_s[kh, r0:r0 + BAND_KEYS, :], bias)

    def values(kh, r0, pt):
        lead = r0 % LANES
        return _band_values(vt_s[kh, :, r0 - lead:r0 - lead + 2 * LANES], lead, pt)

    emitted = 0
    st_next = [scores(*b) for b in stages[0]]
    for i, stage in enumerate(stages):
        st = st_next
        while side and emitted < (i + 1) * side_per_stage and side[0][0] == COLS:
            cost, emit = side.pop(0)
            emit()
            emitted += cost
        if i + 1 < len(stages):
            st_next = [scores(*b) for b in stages[i + 1]]
        soft = [_band_softmax(s, kh, sink_of) for s, (kh, _) in zip(st, stage)]
        while side and emitted < (i + 1) * side_per_stage:
            cost, emit = side.pop(0)
            emit()
            emitted += cost
        outs = [values(kh, r0, pt) for (pt, _), (kh, r0) in zip(soft, stage)]
        for ot, (_, inv_den), (kh, r0) in zip(outs, soft, stage):
            _band_store(a_s, r0, kh, ot, inv_den)
    for _, emit in side:
        emit()

    kprev_s[nxt] = k4_s[:, tile:tile + WINDOW, :]
    vtprev_s[nxt] = vt_s[:, :, tile:tile + WINDOW]

    _merge(a_s, boin_s, gate_s, mix_s, x_ref, xo_ref, wao_ref, wco_ref, wmo_ref, norm_next_tile)


def _mixer_sample_kernel(slopes_ref, sink_ref, x_ref, g_ref, kc_ref, vc_ref, st_ref,
                         win_ref, convw_ref, wao_ref, wco_ref, wmo_ref,
                         xo_ref, ko_ref, vo_ref, co_ref,
                         h_s, q_s, k4_s, v4_s, a_s, u_s, boin_s, gate_s, mix_s, bias_s, *, layer, seqs, rows):
    wqkv_ref, wconv_ref, wgate_ref = _split_w_in(win_ref)
    keys = WINDOW + rows

    @pl.when(pl.program_id(0) == 0)
    def _():
        _build_score_bias(bias_s, slopes_ref, sink_ref, layer, rows, keys, (0,))

    u_s[:, SUBLANES - CONV_LEFT:SUBLANES, :] = st_ref[...]

    h_s[...] = _rms(x_ref[...], g_ref[...]).astype(BF16)
    for c in range(0, D_MODEL, COLS):
        _q_chunk(c, h_s, q_s, wqkv_ref, ATTN_SCALE)
    k, v = _project_kv(h_s, wqkv_ref)
    ko_ref[...] = k
    vo_ref[...] = v

    def store_cache(dst_ref, head, dup):
        dup = dup.reshape(seqs, WINDOW, LANES).astype(BF16)
        dst_ref[head, :, 0:WINDOW, 0:LANES] = dup
        dst_ref[head, :, 0:WINDOW, LANES:] = dup

    def store_new(dst_ref, head, dup):
        dup = dup.reshape(seqs, rows, LANES).astype(BF16)
        dst_ref[head, :, WINDOW:keys, 0:LANES] = dup
        dst_ref[head, :, WINDOW:keys, LANES:] = dup

    _store_lane_tiled(k4_s, kc_ref[...].reshape(seqs * WINDOW, KV_DIM), store_cache)
    _store_lane_tiled(v4_s, vc_ref[...].reshape(seqs * WINDOW, KV_DIM), store_cache)
    _store_lane_tiled(k4_s, k, store_new)
    _store_lane_tiled(v4_s, v, store_new)

    units = [
        _group_attention_unit(
            q_s, a_s, g * rows, rows, kh,
            functools.partial(lambda kh, g: k4_s[kh, g], kh, g),
            functools.partial(lambda kh, g: v4_s[kh, g], kh, g),
            functools.partial(lambda kh: bias_s[0, kh * GROUP * rows:(kh + 1) * GROUP * rows, :], kh))
        for kh in range(N_KV_HEADS) for g in range(seqs)]
    side = [(3 * COLS, functools.partial(_conv_chunk, c, h_s, boin_s, u_s, rows, co_ref, wconv_ref, convw_ref))
            for c in range(0, D_MODEL, COLS)]
    side += [(2 * COLS, functools.partial(_gate_chunk, c, h_s, gate_s, wgate_ref)) for c in range(0, D_MODEL, COLS)]
    _emit_pipelined(units, side)
    _merge(a_s, boin_s, gate_s, mix_s, x_ref, xo_ref, wao_ref, wco_ref, wmo_ref)


def _cross_attention_units(cq_s, co_s, mk_ref, mv_ref, groups, rows):
    lo = _low_half((rows, LANES))
    zero = jnp.zeros((rows, LANES), BF16)
    stacked = MEM_HEADS * rows <= 2 * LANES
    units = []
    for g in range(groups):
        rs = slice(g * rows, (g + 1) * rows)
        if stacked:
            def scores(rs=rs, g=g):
                return _dot_nt(_head_rows(cq_s[rs, 0:LANES], cq_s[rs, LANES:]), mk_ref[g].astype(BF16))

            def values(p, rs=rs, g=g):
                oa, ob = _pick_heads(_dot(p.astype(BF16), mv_ref[g].astype(BF16)), rows)
                co_s[rs, 0:LANES] = oa.astype(BF16)
                co_s[rs, LANES:] = ob.astype(BF16)

            units.append((scores, _softmax, values))
            continue
        held = {}
        for hh in range(MEM_HEADS):
            c, odd = divmod(hh, 2)

            def scores(rs=rs, g=g, c=c, odd=odd):
                qcol = cq_s[rs, c * LANES:(c + 1) * LANES]
                qm = jnp.where(lo, zero, qcol) if odd else jnp.where(lo, qcol, zero)
                qm = jnp.concatenate([qm, zero] if c == 0 else [zero, qm], axis=1)
                return _dot_nt(qm, mk_ref[g].astype(BF16))

            def values(p, rs=rs, g=g, c=c, odd=odd, held=held):
                out = _dot(p.astype(BF16), mv_ref[g].astype(BF16))[:, c * LANES:(c + 1) * LANES]
                if odd:
                    co_s[rs, c * LANES:(c + 1) * LANES] = jnp.where(lo, held.pop(c), out).astype(BF16)
                else:
                    held[c] = out

            units.append((scores, _softmax, values))
    return units


def _ffn_kernel(x_ref, mk_ref, mv_ref, gc_ref, wcq_ref, wcout_ref, gm_ref, wup_ref, wdown_ref, gf_ref,
                xo_ref,
                h_s, cq_s, co_s, x1_s, hm_s, act_s, *, groups, rows, final):
    i = pl.program_id(0)
    cur = i % 2
    nxt = 1 - cur

    @pl.when(i == 0)
    def _():
        x1_s[cur] = jnp.zeros(x1_s.shape[1:], F32)
        hm_s[cur] = jnp.zeros(hm_s.shape[1:], BF16)

    def norm_in():
        h_s[...] = _rms(x_ref[...], gc_ref[...]).astype(BF16)

    def project_q():
        cq_s[...] = (_dot(h_s[...], wcq_ref[...]) * MEM_SCALE).astype(BF16)

    def residual(c):
        cs = slice(c, c + COLS)
        x1_s[nxt, :, cs] = x_ref[:, cs] + _dot(co_s[...], wcout_ref[:, cs])

    def norm_mlp():
        hm_s[nxt] = _rms(x1_s[nxt], gm_ref[...]).astype(BF16)

    units = _cross_attention_units(cq_s, co_s, mk_ref, mv_ref, groups, rows)
    pieces = [norm_in, project_q]
    state = {}
    for j in range(len(units) + 2):
        def piece(j=j):
            if j < len(units):
                state[j] = units[j][0]()
            if 0 <= j - 1 < len(units):
                state[j - 1] = units[j - 1][1](state[j - 1])
            if 0 <= j - 2 < len(units):
                units[j - 2][2](state.pop(j - 2))
        pieces.append(piece)
    half = D_MODEL // 2
    pieces += [lambda: [residual(c) for c in range(0, half, COLS)],
               lambda: [residual(c) for c in range(half, D_MODEL, COLS)],
               norm_mlp]

    for c in range(0, D_FF, COLS):
        up = jnp.maximum(_dot(hm_s[cur], wup_ref[:, c:c + COLS]), 0.0)
        act_s[:, c:c + COLS] = (up * up).astype(BF16)
        if pieces:
            pieces.pop(0)()
    for piece in pieces:
        piece()
    for c in range(0, D_MODEL, COLS):
        cs = slice(c, c + COLS)
        x1_s[cur, :, cs] = x1_s[cur, :, cs] + _dot(act_s[...], wdown_ref[:, cs])
    if final:
        xo_ref[...] = _rms(x1_s[cur], gf_ref[...])
    else:
        xo_ref[...] = x1_s[cur]


def _memkv_kernel(mem_ref, g_ref, w_ref, mk_ref, mv_ref):
    for b in range(mem_ref.shape[0]):
        h = _rms(mem_ref[b], g_ref[...]).astype(BF16)
        kv = _dot(h, w_ref[...])
        mk_ref[b] = kv[:, 0:MEM_DIM]
        mv_ref[b] = kv[:, MEM_DIM:]


def _params(n_axes):
    return pltpu.CompilerParams(dimension_semantics=("arbitrary",) * n_axes, vmem_limit_bytes=VMEM_LIMIT)


def _resident(shape, layer):
    zeros = (0,) * len(shape)
    return pl.BlockSpec((None,) + tuple(shape), lambda *_: (layer,) + zeros, pipeline_mode=pl.Buffered(1))


def _smem():
    return pl.BlockSpec(memory_space=pltpu.SMEM)


def _split_w_in(win_ref):
    return (win_ref.at[:, 0:IN_CONV], win_ref.at[:, IN_CONV:IN_GATE], win_ref.at[:, IN_GATE:IN_COLS])


def _mixer_weight_specs(layer):
    return [
        _resident((D_MODEL, IN_COLS), layer),
        _resident((CONV_TAPS, D_MODEL), layer),
        _resident((D_MODEL, D_MODEL), layer),
        _resident((D_MODEL, D_MODEL), layer),
        _resident((D_MODEL, D_MODEL), layer),
    ]


def _mixer_weights(w):
    return (w["w_in"], w["conv_w"], w["w_attn_out"], w["w_conv_out"], w["w_mix_out"])


def _mixer_prompt(layer, x, slopes, w):
    batch, seq, _ = x.shape
    tile = PROMPT_TILE
    row = lambda b, t: (b, t, 0)
    next_row = lambda b, t: (b, jnp.minimum(t + 1, seq // tile - 1), 0)
    per_seq = lambda b, t: (b, 0, 0)
    return pl.pallas_call(
        functools.partial(_mixer_prompt_kernel, layer=layer, tile=tile),
        grid=(batch, seq // tile),
        in_specs=[_smem(), _smem(),
                  pl.BlockSpec((None, tile, D_MODEL), row),
                  pl.BlockSpec((None, tile, D_MODEL), next_row),
                  _resident((1, D_MODEL), layer)] + _mixer_weight_specs(layer),
        out_specs=[pl.BlockSpec((None, tile, D_MODEL), row),
                   pl.BlockSpec((None, WINDOW, KV_DIM), per_seq),
                   pl.BlockSpec((None, WINDOW, KV_DIM), per_seq),
                   pl.BlockSpec((1, CONV_LEFT, D_MODEL), per_seq)],
        out_shape=[jax.ShapeDtypeStruct(x.shape, F32),
                   jax.ShapeDtypeStruct((batch, WINDOW, KV_DIM), F32),
                   jax.ShapeDtypeStruct((batch, WINDOW, KV_DIM), F32),
                   jax.ShapeDtypeStruct((batch, CONV_LEFT, D_MODEL), F32)],
        scratch_shapes=[
            pltpu.VMEM((2, tile, D_MODEL), BF16),
            pltpu.VMEM((tile, D_MODEL), BF16),
            pltpu.VMEM((N_KV_HEADS, WINDOW + tile, 2 * LANES), BF16),
            pltpu.VMEM((N_KV_HEADS, HEAD_DIM, WINDOW + tile), BF16),
            pltpu.VMEM((tile, D_MODEL), BF16),
            pltpu.VMEM((1, SUBLANES + tile, D_MODEL), F32),
            pltpu.VMEM((tile, D_MODEL), BF16),
            pltpu.VMEM((2, tile, D_MODEL), F32),
            pltpu.VMEM((tile, D_MODEL), BF16),
            pltpu.VMEM((START_VARIANTS, N_KV_HEADS * BAND_KEYS, 2 * LANES), F32),
            pltpu.VMEM((2, N_KV_HEADS, WINDOW, 2 * LANES), BF16),
            pltpu.VMEM((2, N_KV_HEADS, HEAD_DIM, WINDOW), BF16),
            pltpu.VMEM((2, 1, SUBLANES, D_MODEL), F32),
        ],
        compiler_params=_params(2),
        name=f"mixer_prompt_l{layer}",
    )(slopes, w["sink"], x, x, w["g_mix"], *_mixer_weights(w))


def _mixer_sample(layer, x2, slopes, kc, vc, st, w, rows):
    total = x2.shape[0]
    seqs = SAMPLE_SEQS
    tile = seqs * rows
    keys = WINDOW + rows
    row = lambda i: (i, 0)
    per_seq = lambda i: (layer, i, 0, 0)
    return pl.pallas_call(
        functools.partial(_mixer_sample_kernel, layer=layer, seqs=seqs, rows=rows),
        grid=(total // tile,),
        in_specs=[_smem(), _smem(),
                  pl.BlockSpec((tile, D_MODEL), row),
                  _resident((1, D_MODEL), layer),
                  pl.BlockSpec((None, seqs, WINDOW, KV_DIM), per_seq),
                  pl.BlockSpec((None, seqs, WINDOW, KV_DIM), per_seq),
                  pl.BlockSpec((None, seqs, CONV_LEFT, D_MODEL), per_seq),
                  ] + _mixer_weight_specs(layer),
        out_specs=[pl.BlockSpec((tile, D_MODEL), row),
                   pl.BlockSpec((tile, KV_DIM), row),
                   pl.BlockSpec((tile, KV_DIM), row),
                   pl.BlockSpec((seqs, CONV_LEFT, D_MODEL), lambda i: (i, 0, 0))],
        out_shape=[jax.ShapeDtypeStruct(x2.shape, F32),
                   jax.ShapeDtypeStruct((total, KV_DIM), F32),
                   jax.ShapeDtypeStruct((total, KV_DIM), F32),
                   jax.ShapeDtypeStruct((total // rows, CONV_LEFT, D_MODEL), F32)],
        scratch_shapes=[
            pltpu.VMEM((tile, D_MODEL), BF16),
            pltpu.VMEM((tile, D_MODEL), BF16),
            pltpu.VMEM((N_KV_HEADS, seqs, keys, 2 * LANES), BF16),
            pltpu.VMEM((N_KV_HEADS, seqs, keys, 2 * LANES), BF16),
            pltpu.VMEM((tile, D_MODEL), BF16),
            pltpu.VMEM((seqs, SUBLANES + rows, D_MODEL), F32),
            pltpu.VMEM((tile, D_MODEL), BF16),
            pltpu.VMEM((2, tile, D_MODEL), F32),
            pltpu.VMEM((tile, D_MODEL), BF16),
            pltpu.VMEM((1, N_HEADS * rows, SCORE_COLS), F32),
        ],
        compiler_params=_params(1),
        name=f"mixer_sample_l{layer}",
    )(slopes, w["sink"], x2, w["g_mix"], kc, vc, st, *_mixer_weights(w))


def _ffn(layer, x2, mk, mv, w, groups, rows, final, name):
    total = x2.shape[0]
    tile = groups * rows
    steps_per_group = (total // mk.shape[1]) // rows if groups == 1 else 1
    n_tiles = total // tile
    tile_in = lambda i: jnp.minimum(i, n_tiles - 1)
    mem_spec = pl.BlockSpec((None, groups, N_MEM, MEM_DIM),
                            lambda i: (layer, tile_in(i) // steps_per_group, 0, 0))
    return pl.pallas_call(
        functools.partial(_ffn_kernel, groups=groups, rows=rows, final=final),
        grid=(n_tiles + 1,),
        in_specs=[pl.BlockSpec((tile, D_MODEL), lambda i: (tile_in(i), 0)), mem_spec, mem_spec,
                  _resident((1, D_MODEL), layer),
                  _resident((D_MODEL, MEM_DIM), layer),
                  _resident((MEM_DIM, D_MODEL), layer),
                  _resident((1, D_MODEL), layer),
                  _resident((D_MODEL, D_FF), layer),
                  _resident((D_FF, D_MODEL), layer),
                  pl.BlockSpec((1, D_MODEL), lambda i: (0, 0))],
        out_specs=pl.BlockSpec((tile, D_MODEL), lambda i: (jnp.maximum(i - 1, 0), 0)),
        out_shape=jax.ShapeDtypeStruct(x2.shape, F32),
        scratch_shapes=[
            pltpu.VMEM((tile, D_MODEL), BF16),
            pltpu.VMEM((tile, MEM_DIM), BF16),
            pltpu.VMEM((tile, MEM_DIM), BF16),
            pltpu.VMEM((2, tile, D_MODEL), F32),
            pltpu.VMEM((2, tile, D_MODEL), BF16),
            pltpu.VMEM((tile, D_FF), BF16),
        ],
        compiler_params=_params(1),
        name=name,
    )(x2, mk, mv, w["g_cross"], w["w_cq"], w["w_co"], w["g_mlp"], w["w_up"], w["w_down"], w["g_final"])


def _memory_kv(mem, g_mem, w_ckv):
    batch = mem.shape[0]
    out = jax.ShapeDtypeStruct((DEPTH, batch, N_MEM, MEM_DIM), F32)
    return pl.pallas_call(
        _memkv_kernel,
        grid=(DEPTH,),
        in_specs=[pl.BlockSpec((batch, N_MEM, D_MODEL), lambda l: (0, 0, 0)),
                  pl.BlockSpec((None, 1, D_MODEL), lambda l: (l, 0, 0)),
                  pl.BlockSpec((None, D_MODEL, 2 * MEM_DIM), lambda l: (l, 0, 0))],
        out_specs=[pl.BlockSpec((None, batch, N_MEM, MEM_DIM), lambda l: (l, 0, 0, 0))] * 2,
        out_shape=[out, out],
        compiler_params=_params(1),
        name="memory_kv",
    )(mem, g_mem, w_ckv)


def kernel(x_prompt, x_sample, mem_prompt, cache_attn_k, cache_attn_v, state_conv, cache_mem_k, cache_mem_v,
           norm_mix_g, w_in, conv_w, attn_sink, w_attn_out, w_conv_out, w_mix_out, norm_cross_g, norm_mem_g,
           w_cq, w_ckv, w_co, norm_mlp_g, w_up, w_down, norm_final_g):
    batch, seq, _ = x_prompt.shape
    dec_batch, dec_seq, _ = x_sample.shape
    assert seq % PROMPT_TILE == 0 and PROMPT_TILE % LANES == 0 and PROMPT_TILE >= WINDOW
    assert seq % FFN_TILE == 0
    assert dec_seq <= CHUNK and dec_seq % 16 == 0 and dec_batch % SAMPLE_SEQS == 0
    assert cache_attn_k.shape[2] == WINDOW

    w = {
        "sink": attn_sink,
        "g_mix": norm_mix_g[:, None, :],
        "w_in": w_in.astype(BF16),
        "conv_w": conv_w,
        "w_attn_out": w_attn_out.astype(BF16),
        "w_conv_out": w_conv_out.astype(BF16),
        "w_mix_out": w_mix_out.astype(BF16),
        "g_cross": norm_cross_g[:, None, :],
        "w_cq": w_cq.astype(BF16),
        "w_co": w_co.astype(BF16),
        "g_mlp": norm_mlp_g[:, None, :],
        "w_up": w_up.astype(BF16),
        "w_down": w_down.astype(BF16),
        "g_final": norm_final_g[None, :],
    }
    slopes = jnp.exp2(-8.0 * jnp.arange(1, N_HEADS + 1, dtype=F32) / N_HEADS)

    mk_p, mv_p = _memory_kv(mem_prompt, norm_mem_g[:, None, :], w_ckv.astype(BF16))

    kc = cache_attn_k.reshape(DEPTH, dec_batch, WINDOW, KV_DIM)
    vc = cache_attn_v.reshape(DEPTH, dec_batch, WINDOW, KV_DIM)
    mk_s = cache_mem_k.reshape(DEPTH, dec_batch, N_MEM, MEM_DIM)
    mv_s = cache_mem_v.reshape(DEPTH, dec_batch, N_MEM, MEM_DIM)

    xp = x_prompt
    xs = x_sample.reshape(dec_batch * dec_seq, D_MODEL)
    kp_l, vp_l, cp_l, ks_l, vs_l, cs_l = [], [], [], [], [], []
    for layer in range(DEPTH):
        final = layer == DEPTH - 1
        xp, kp, vp, cp = _mixer_prompt(layer, xp, slopes, w)
        xp = _ffn(layer, xp.reshape(batch * seq, D_MODEL), mk_p, mv_p, w, 1, FFN_TILE, final,
                  f"ffn_prompt_l{layer}").reshape(batch, seq, D_MODEL)
        xs, kn, vn, cs = _mixer_sample(layer, xs, slopes, kc, vc, state_conv, w, dec_seq)
        xs = _ffn(layer, xs, mk_s, mv_s, w, SAMPLE_SEQS, dec_seq, final, f"ffn_sample_l{layer}")
        kp_l.append(kp)
        vp_l.append(vp)
        cp_l.append(cp)
        ks_l.append(kn)
        vs_l.append(vn)
        cs_l.append(cs)

    kv_p = (DEPTH, batch, WINDOW, N_KV_HEADS, HEAD_DIM)
    kv_s = (DEPTH, dec_batch, dec_seq, N_KV_HEADS, HEAD_DIM)
    mem_shape = (DEPTH, batch, N_MEM, MEM_HEADS, MEM_DIM // MEM_HEADS)
    return (xp, xs.reshape(dec_batch, dec_seq, D_MODEL),
            jnp.stack(kp_l).reshape(kv_p), jnp.stack(vp_l).reshape(kv_p), jnp.stack(cp_l),
            mk_p.reshape(mem_shape), mv_p.reshape(mem_shape),
            jnp.stack(ks_l).reshape(kv_s), jnp.stack(vs_l).reshape(kv_s), jnp.stack(cs_l))
```

```python
import functools

import jax
import jax.numpy as jnp
from jax import lax
from jax.experimental import pallas as pl
from jax.experimental.pallas import tpu as pltpu

D_MODEL = 1024
DEPTH = 4
CHUNK = 64
WINDOW = 128
WIN_CHUNKS = WINDOW // CHUNK
HEAD_DIM = 64
N_HEADS = 16
N_KV_HEADS = 4
GROUP = N_HEADS // N_KV_HEADS
KV_DIM = N_KV_HEADS * HEAD_DIM
N_MEM = 256
MEM_HEADS = 4
MEM_DIM = 256
D_FF = 4 * D_MODEL
IN_CONV = N_HEADS * HEAD_DIM + 2 * KV_DIM
IN_GATE = IN_CONV + 3 * D_MODEL
IN_COLS = IN_GATE + 2 * D_MODEL
EPS = 1e-6
ATTN_SCALE = HEAD_DIM ** -0.5
MEM_SCALE = (MEM_DIM // MEM_HEADS) ** -0.5
CONV_TAPS = 3
CONV_LEFT = CONV_TAPS - 1

LANES = 128
SUBLANES = 8
HALF = LANES // 2
SCORE_COLS = 2 * LANES
MASKED = -1e30
V7X_VMEM_BYTES = 64 * 1024 * 1024
VMEM_LIMIT = V7X_VMEM_BYTES - 8 * 1024 * 1024

PROMPT_TILE = 512
FFN_TILE = 512
SAMPLE_SEQS = 8
COLS = 256
BF16 = jnp.bfloat16
F32 = jnp.float32


def _dot(a, b):
    return jnp.dot(a, b, preferred_element_type=F32)


def _dot_nt(a, b):
    return lax.dot_general(a, b, (((1,), (1,)), ((), ())), preferred_element_type=F32)


def _rms(x, g):
    return x * lax.rsqrt(jnp.mean(x * x, axis=-1, keepdims=True) + EPS) * g


def _low_half(shape):
    return lax.broadcasted_iota(jnp.int32, shape, len(shape) - 1) < HALF


def _dup_halves(col):
    swapped = pltpu.roll(col, HALF, axis=1)
    lo = _low_half(col.shape)
    return jnp.where(lo, col, swapped), jnp.where(lo, swapped, col)


def _store_lane_tiled(dst_ref, kv, store):
    for c in range(KV_DIM // LANES):
        even, odd = _dup_halves(kv[:, c * LANES:(c + 1) * LANES])
        store(dst_ref, 2 * c, even)
        store(dst_ref, 2 * c + 1, odd)


def _pad_keys(block):
    pad = jnp.zeros((SCORE_COLS - block.shape[0], block.shape[1]), block.dtype)
    return jnp.concatenate([block, pad], axis=0)


def _head_rows(qa, qb):
    lo = _low_half(qa.shape)
    zero = jnp.zeros_like(qa)
    blocks = (
        jnp.concatenate([jnp.where(lo, qa, zero), zero], axis=1),
        jnp.concatenate([jnp.where(lo, zero, qa), zero], axis=1),
        jnp.concatenate([zero, jnp.where(lo, qb, zero)], axis=1),
        jnp.concatenate([zero, jnp.where(lo, zero, qb)], axis=1),
    )
    return jnp.concatenate(blocks, axis=0)


def _pick_heads(o4, rows):
    lo = _low_half((rows, LANES))
    oa = jnp.where(lo, o4[0:rows, 0:LANES], o4[rows:2 * rows, 0:LANES])
    ob = jnp.where(lo, o4[2 * rows:3 * rows, LANES:], o4[3 * rows:4 * rows, LANES:])
    return oa, ob


def _build_score_bias(bias_ref, slopes_ref, sink_ref, layer, nq, nk, first_valid_keys):
    qi = lax.broadcasted_iota(jnp.int32, (nq, SCORE_COLS), 0)
    ji = lax.broadcasted_iota(jnp.int32, (nq, SCORE_COLS), 1)
    dist = jnp.abs(qi + WINDOW - ji).astype(F32)
    for h in range(N_HEADS):
        base = jnp.where(ji == nk, sink_ref[layer, h], -slopes_ref[h] * dist)
        for v, first in enumerate(first_valid_keys):
            valid = (ji >= first) & (ji <= nk)
            bias_ref[v, h * nq:(h + 1) * nq, :] = jnp.where(valid, base, MASKED)


def _softmax(s):
    p = jnp.exp(s - jnp.max(s, axis=-1, keepdims=True))
    return p * (1.0 / jnp.sum(p, axis=-1, keepdims=True))


def _group_attention_unit(q_s, a_s, r0, nq, kv_head, kblock_of, vblock_of, bias_of):
    c0 = kv_head * GROUP * HEAD_DIM

    def scores():
        qs = _head_rows(q_s[r0:r0 + nq, c0:c0 + LANES], q_s[r0:r0 + nq, c0 + LANES:c0 + 2 * LANES])
        return _dot_nt(qs, _pad_keys(kblock_of())) + bias_of()

    def values(p):
        oa, ob = _pick_heads(_dot(p.astype(BF16), _pad_keys(vblock_of())), nq)
        a_s[r0:r0 + nq, c0:c0 + LANES] = oa.astype(BF16)
        a_s[r0:r0 + nq, c0 + LANES:c0 + 2 * LANES] = ob.astype(BF16)

    return scores, _softmax, values


def _emit_pipelined(units, side):
    slots = len(units) + 2
    total = sum(cost for cost, _ in side)
    emitted = 0
    state = {}
    for j in range(slots):
        if j < len(units):
            state[j] = units[j][0]()
        if 0 <= j - 1 < len(units):
            state[j - 1] = units[j - 1][1](state[j - 1])
        while side and emitted * slots < (j + 1) * total:
            cost, emit = side.pop(0)
            emit()
            emitted += cost
        if 0 <= j - 2 < len(units):
            units[j - 2][2](state.pop(j - 2))
    for _, emit in side:
        emit()


def _q_chunk(c, h_s, q_s, wqkv_ref, scale):
    q_s[:, c:c + COLS] = (_dot(h_s[...], wqkv_ref[:, c:c + COLS]) * scale).astype(BF16)


def _project_kv(h_s, wqkv_ref):
    hb = h_s[...]
    k = _dot(hb, wqkv_ref[:, D_MODEL:D_MODEL + KV_DIM])
    v = _dot(hb, wqkv_ref[:, D_MODEL + KV_DIM:D_MODEL + 2 * KV_DIM])
    return k, v


def _gate_chunk(c, h_s, gate_s, wgate_ref):
    hb = h_s[...]
    gate_s[0, :, c:c + COLS] = jax.nn.sigmoid(_dot(hb, wgate_ref[:, c:c + COLS]))
    gate_s[1, :, c:c + COLS] = jax.nn.sigmoid(_dot(hb, wgate_ref[:, D_MODEL + c:D_MODEL + c + COLS]))


def _conv_chunk(c, h_s, boin_s, u3, rows, co_ref, wconv_ref, convw_ref, next_left_ref=None):
    groups = u3.shape[0]
    total = groups * rows
    cs = slice(c, c + COLS)
    hb = h_s[...]
    ch = _dot(hb, wconv_ref[:, c:c + COLS])
    cc = _dot(hb, wconv_ref[:, 2 * D_MODEL + c:2 * D_MODEL + c + COLS])
    u = cc * ch
    u3[:, SUBLANES:SUBLANES + rows, cs] = u.reshape(groups, rows, COLS)
    um2 = u3[:, SUBLANES - 2:SUBLANES - 2 + rows, cs].reshape(total, COLS)
    um1 = u3[:, SUBLANES - 1:SUBLANES - 1 + rows, cs].reshape(total, COLS)
    conv = um2 * convw_ref[0:1, cs] + um1 * convw_ref[1:2, cs] + u * convw_ref[2:3, cs]
    cb = _dot(hb, wconv_ref[:, D_MODEL + c:D_MODEL + c + COLS])
    boin_s[:, cs] = (cb * conv).astype(BF16)
    tail = u3[:, SUBLANES + rows - CONV_LEFT:SUBLANES + rows, cs]
    co_ref[:, :, cs] = tail
    if next_left_ref is not None:
        next_left_ref[:, SUBLANES - CONV_LEFT:SUBLANES, cs] = tail


def _merge(a_s, boin_s, gate_s, mix_s, x_ref, xo_ref, wao_ref, wco_ref, wmo_ref, anchor=None):
    for c in range(0, D_MODEL, COLS):
        cs = slice(c, c + COLS)
        a = _dot(a_s[...], wao_ref[:, cs])
        bo = _dot(boin_s[...], wco_ref[:, cs])
        mixed = (gate_s[0, :, cs] * a + gate_s[1, :, cs] * bo).astype(BF16)
        if anchor is not None:
            never, value = anchor
            mixed = jnp.where(never, value[:, cs], mixed)
        mix_s[:, cs] = mixed
    for c in range(0, D_MODEL, COLS):
        cs = slice(c, c + COLS)
        xo_ref[:, cs] = x_ref[:, cs] + _dot(mix_s[...], wmo_ref[:, cs])


BAND_HEAD_ORDER = (0, 2, 1, 3)
BAND_KEYS = WINDOW + CHUNK
START_VARIANTS = WIN_CHUNKS + 1
BLOCKS_PER_STAGE = 2
LOG2E = 1.4426950408889634
BAND_Q_SCALE = ATTN_SCALE * LOG2E


def _build_band_bias(bias_ref, slopes_ref):
    ji = lax.broadcasted_iota(jnp.int32, (BAND_KEYS, 2 * LANES), 0)
    li = lax.broadcasted_iota(jnp.int32, (BAND_KEYS, 2 * LANES), 1)
    dist = jnp.abs(li % CHUNK + WINDOW - ji).astype(F32)
    slot = li // CHUNK
    for kh in range(N_KV_HEADS):
        slope = jnp.zeros((BAND_KEYS, 2 * LANES), F32)
        for s, g in enumerate(BAND_HEAD_ORDER):
            slope = jnp.where(slot == s, slopes_ref[kh * GROUP + g], slope)
        base = -slope * dist * LOG2E
        rows = slice(kh * BAND_KEYS, (kh + 1) * BAND_KEYS)
        for v in range(START_VARIANTS):
            bias_ref[v, rows, :] = jnp.where(ji >= (WIN_CHUNKS - v) * CHUNK, base, MASKED)


def _band_query_rows(qa, qb):
    lo = _low_half(qa.shape)
    zero = jnp.zeros_like(qa)
    blocks = (
        jnp.concatenate([jnp.where(lo, qa, zero), zero], axis=1),
        jnp.concatenate([zero, jnp.where(lo, qb, zero)], axis=1),
        jnp.concatenate([jnp.where(lo, zero, qa), zero], axis=1),
        jnp.concatenate([zero, jnp.where(lo, zero, qb)], axis=1),
    )
    return jnp.concatenate(blocks, axis=0)


def _band_scores(q_s, r0, kv_head, kblock, bias):
    c0 = kv_head * GROUP * HEAD_DIM
    qs = _band_query_rows(q_s[r0:r0 + CHUNK, c0:c0 + LANES], q_s[r0:r0 + CHUNK, c0 + LANES:c0 + 2 * LANES])
    return _dot_nt(kblock, qs) + bias


def _band_softmax(st, kv_head, sink_of):
    lo = _low_half((1, LANES))
    probs, inv_den = [], []
    for col in range(2):
        s = st[:, col * LANES:(col + 1) * LANES]
        ha, hb = BAND_HEAD_ORDER[2 * col], BAND_HEAD_ORDER[2 * col + 1]
        sink = jnp.where(lo, sink_of(kv_head * GROUP + ha), sink_of(kv_head * GROUP + hb)) * LOG2E
        m = jnp.maximum(jnp.max(s, axis=0, keepdims=True), sink)
        p = jnp.exp2(s - m)
        inv_den.append(1.0 / (jnp.sum(p, axis=0, keepdims=True) + jnp.exp2(sink - m)))
        probs.append(p.astype(BF16))
    return jnp.concatenate(probs, axis=1), inv_den


def _band_values(vt_block, lead, pt):
    parts = [pt]
    if lead:
        parts.insert(0, jnp.zeros((lead, 2 * LANES), BF16))
    if vt_block.shape[1] > lead + BAND_KEYS:
        parts.append(jnp.zeros((vt_block.shape[1] - lead - BAND_KEYS, 2 * LANES), BF16))
    return _dot(vt_block, jnp.concatenate(parts, axis=0))


def _band_store(a_s, r0, kv_head, ot, inv_den):
    c0 = kv_head * GROUP * HEAD_DIM
    stacked = jnp.concatenate([ot[:, 0:LANES] * inv_den[0], ot[:, LANES:] * inv_den[1]], axis=0)
    out = stacked.T.astype(BF16)
    a_s[r0:r0 + CHUNK, c0:c0 + LANES] = out[0:CHUNK]
    a_s[r0:r0 + CHUNK, c0 + LANES:c0 + 2 * LANES] = out[CHUNK:]


def _mixer_prompt_kernel(slopes_ref, sink_ref, x_ref, xn_ref, g_ref, win_ref, convw_ref,
                         wao_ref, wco_ref, wmo_ref,
                         xo_ref, ko_ref, vo_ref, co_ref,
                         h2_s, q_s, k4_s, vt_s, a_s, u_s, boin_s, gate_s, mix_s, bias_s,
                         kprev_s, vtprev_s, uprev_s, *, layer, tile):
    wqkv_ref, wconv_ref, wgate_ref = _split_w_in(win_ref)
    t = pl.program_id(1)
    cur = t % 2
    nxt = 1 - cur

    @pl.when(t == 0)
    def _():
        _build_band_bias(bias_s, slopes_ref)
        kprev_s[cur] = jnp.zeros(kprev_s.shape[1:], BF16)
        vtprev_s[cur] = jnp.zeros(vtprev_s.shape[1:], BF16)
        uprev_s[cur] = jnp.zeros(uprev_s.shape[1:], F32)
        h2_s[cur] = _rms(x_ref[...], g_ref[...]).astype(BF16)

    h_s = h2_s.at[cur]


    k4_s[:, 0:WINDOW, :] = kprev_s[cur]
    vt_s[:, :, 0:WINDOW] = vtprev_s[cur]
    left = slice(SUBLANES - CONV_LEFT, SUBLANES)
    u_s[:, left, :] = uprev_s[cur, :, left, :]

    cols = range(0, D_MODEL, COLS)
    side = [(3 * COLS, functools.partial(_conv_chunk, c, h_s, boin_s, u_s, tile, co_ref, wconv_ref, convw_ref,
                                         uprev_s.at[nxt])) for c in cols]
    side += [(2 * COLS, functools.partial(_gate_chunk, c, h_s, gate_s, wgate_ref)) for c in cols]

    k, v = _project_kv(h_s, wqkv_ref)
    for c in cols:
        _q_chunk(c, h_s, q_s, wqkv_ref, BAND_Q_SCALE)
    side.pop(0)[1]()
    ko_ref[...] = k[tile - WINDOW:, :]
    vo_ref[...] = v[tile - WINDOW:, :]

    def store_new(dst_ref, head, dup):
        dup = dup.astype(BF16)
        dst_ref[head, WINDOW:WINDOW + tile, 0:LANES] = dup
        dst_ref[head, WINDOW:WINDOW + tile, LANES:] = dup

    _store_lane_tiled(k4_s, k, store_new)
    vt = v.T.astype(BF16)
    for kh in range(N_KV_HEADS):
        vt_s[kh, :, WINDOW:WINDOW + tile] = vt[kh * HEAD_DIM:(kh + 1) * HEAD_DIM, :]

    blocks = [(kh, r0) for kh in range(N_KV_HEADS) for r0 in range(0, tile, CHUNK)]
    stages = [blocks[i:i + BLOCKS_PER_STAGE] for i in range(0, len(blocks), BLOCKS_PER_STAGE)]
    side_per_stage = sum(cost for cost, _ in side) / len(stages)

    sink_of = lambda h: sink_ref[layer, h]

    def scores(kh, r0):
        chunk = r0 // CHUNK
        variant = jnp.where(t == 0, chunk, WIN_CHUNKS) if chunk < WIN_CHUNKS else WIN_CHUNKS
        bias = bias_s[variant, kh * BAND_KEYS:(kh + 1) * BAND_KEYS, :]
        return _band_scores(q_s, r0, kh, k4_s[kh, r0:r0 + BAND_KEYS, :], bias)

    def values(kh, r0, pt):
        lead = r0 % LANES
        return _band_values(vt_s[kh, :, r0 - lead:r0 - lead + 2 * LANES], lead, pt)

    emitted = 0
    st_next = [scores(*b) for b in stages[0]]
    for i, stage in enumerate(stages):
        st = st_next
        if i + 1 < len(stages):
            st_next = [scores(*b) for b in stages[i + 1]]
        soft = [_band_softmax(s, kh, sink_of) for s, (kh, _) in zip(st, stage)]
        while side and emitted < (i + 1) * side_per_stage:
            cost, emit = side.pop(0)
            emit()
            emitted += cost
        outs = [values(kh, r0, pt) for (pt, _), (kh, r0) in zip(soft, stage)]
        for ot, (_, inv_den), (kh, r0) in zip(outs, soft, stage):
            _band_store(a_s, r0, kh, ot, inv_den)
    for _, emit in side:
        emit()

    kprev_s[nxt] = k4_s[:, tile:tile + WINDOW, :]
    vtprev_s[nxt] = vt_s[:, :, tile:tile + WINDOW]

    h_next = _rms(xn_ref[...], g_ref[...]).astype(BF16)
    h2_s[nxt] = h_next
    _merge(a_s, boin_s, gate_s, mix_s, x_ref, xo_ref, wao_ref, wco_ref, wmo_ref, anchor=(t < 0, h_next))


def _mixer_sample_kernel(slopes_ref, sink_ref, x_ref, g_ref, kc_ref, vc_ref, st_ref,
                         win_ref, convw_ref, wao_ref, wco_ref, wmo_ref,
                         xo_ref, ko_ref, vo_ref, co_ref,
                         h_s, q_s, k4_s, v4_s, a_s, u_s, boin_s, gate_s, mix_s, bias_s, *, layer, seqs, rows):
    wqkv_ref, wconv_ref, wgate_ref = _split_w_in(win_ref)
    keys = WINDOW + rows

    @pl.when(pl.program_id(0) == 0)
    def _():
        _build_score_bias(bias_s, slopes_ref, sink_ref, layer, rows, keys, (0,))

    u_s[:, SUBLANES - CONV_LEFT:SUBLANES, :] = st_ref[...]

    h_s[...] = _rms(x_ref[...], g_ref[...]).astype(BF16)
    for c in range(0, D_MODEL, COLS):
        _q_chunk(c, h_s, q_s, wqkv_ref, ATTN_SCALE)
    k, v = _project_kv(h_s, wqkv_ref)
    ko_ref[...] = k
    vo_ref[...] = v

    def store_cache(dst_ref, head, dup):
        dup = dup.reshape(seqs, WINDOW, LANES).astype(BF16)
        dst_ref[head, :, 0:WINDOW, 0:LANES] = dup
        dst_ref[head, :, 0:WINDOW, LANES:] = dup

    def store_new(dst_ref, head, dup):
        dup = dup.reshape(seqs, rows, LANES).astype(BF16)
        dst_ref[head, :, WINDOW:keys, 0:LANES] = dup
        dst_ref[head, :, WINDOW:keys, LANES:] = dup

    _store_lane_tiled(k4_s, kc_ref[...].reshape(seqs * WINDOW, KV_DIM), store_cache)
    _store_lane_tiled(v4_s, vc_ref[...].reshape(seqs * WINDOW, KV_DIM), store_cache)
    _store_lane_tiled(k4_s, k, store_new)
    _store_lane_tiled(v4_s, v, store_new)

    units = [
        _group_attention_unit(
            q_s, a_s, g * rows, rows, kh,
            functools.partial(lambda kh, g: k4_s[kh, g], kh, g),
            functools.partial(lambda kh, g: v4_s[kh, g], kh, g),
            functools.partial(lambda kh: bias_s[0, kh * GROUP * rows:(kh + 1) * GROUP * rows, :], kh))
        for kh in range(N_KV_HEADS) for g in range(seqs)]
    side = [(3 * COLS, functools.partial(_conv_chunk, c, h_s, boin_s, u_s, rows, co_ref, wconv_ref, convw_ref))
            for c in range(0, D_MODEL, COLS)]
    side += [(2 * COLS, functools.partial(_gate_chunk, c, h_s, gate_s, wgate_ref)) for c in range(0, D_MODEL, COLS)]
    _emit_pipelined(units, side)
    _merge(a_s, boin_s, gate_s, mix_s, x_ref, xo_ref, wao_ref, wco_ref, wmo_ref)


def _cross_attention_units(cq_s, co_s, mk_ref, mv_ref, groups, rows):
    lo = _low_half((rows, LANES))
    zero = jnp.zeros((rows, LANES), BF16)
    stacked = MEM_HEADS * rows <= 2 * LANES
    units = []
    for g in range(groups):
        rs = slice(g * rows, (g + 1) * rows)
        if stacked:
            def scores(rs=rs, g=g):
                return _dot_nt(_head_rows(cq_s[rs, 0:LANES], cq_s[rs, LANES:]), mk_ref[g].astype(BF16))

            def values(p, rs=rs, g=g):
                oa, ob = _pick_heads(_dot(p.astype(BF16), mv_ref[g].astype(BF16)), rows)
                co_s[rs, 0:LANES] = oa.astype(BF16)
                co_s[rs, LANES:] = ob.astype(BF16)

            units.append((scores, _softmax, values))
            continue
        held = {}
        for hh in range(MEM_HEADS):
            c, odd = divmod(hh, 2)

            def scores(rs=rs, g=g, c=c, odd=odd):
                qcol = cq_s[rs, c * LANES:(c + 1) * LANES]
                qm = jnp.where(lo, zero, qcol) if odd else jnp.where(lo, qcol, zero)
                qm = jnp.concatenate([qm, zero] if c == 0 else [zero, qm], axis=1)
                return _dot_nt(qm, mk_ref[g].astype(BF16))

            def values(p, rs=rs, g=g, c=c, odd=odd, held=held):
                out = _dot(p.astype(BF16), mv_ref[g].astype(BF16))[:, c * LANES:(c + 1) * LANES]
                if odd:
                    co_s[rs, c * LANES:(c + 1) * LANES] = jnp.where(lo, held.pop(c), out).astype(BF16)
                else:
                    held[c] = out

            units.append((scores, _softmax, values))
    return units


def _ffn_kernel(x_ref, mk_ref, mv_ref, gc_ref, wcq_ref, wcout_ref, gm_ref, wup_ref, wdown_ref, gf_ref,
                xo_ref,
                h_s, cq_s, co_s, x1_s, hm_s, act_s, *, groups, rows, final):
    i = pl.program_id(0)
    cur = i % 2
    nxt = 1 - cur

    @pl.when(i == 0)
    def _():
        x1_s[cur] = jnp.zeros(x1_s.shape[1:], F32)
        hm_s[cur] = jnp.zeros(hm_s.shape[1:], BF16)

    def norm_in():
        h_s[...] = _rms(x_ref[...], gc_ref[...]).astype(BF16)

    def project_q():
        cq_s[...] = (_dot(h_s[...], wcq_ref[...]) * MEM_SCALE).astype(BF16)

    def residual(c):
        cs = slice(c, c + COLS)
        x1_s[nxt, :, cs] = x_ref[:, cs] + _dot(co_s[...], wcout_ref[:, cs])

    def norm_mlp():
        hm_s[nxt] = _rms(x1_s[nxt], gm_ref[...]).astype(BF16)

    units = _cross_attention_units(cq_s, co_s, mk_ref, mv_ref, groups, rows)
    pieces = [norm_in, project_q]
    state = {}
    for j in range(len(units) + 2):
        def piece(j=j):
            if j < len(units):
                state[j] = units[j][0]()
            if 0 <= j - 1 < len(units):
                state[j - 1] = units[j - 1][1](state[j - 1])
            if 0 <= j - 2 < len(units):
                units[j - 2][2](state.pop(j - 2))
        pieces.append(piece)
    half = D_MODEL // 2
    pieces += [lambda: [residual(c) for c in range(0, half, COLS)],
               lambda: [residual(c) for c in range(half, D_MODEL, COLS)],
               norm_mlp]

    for c in range(0, D_FF, COLS):
        up = jnp.maximum(_dot(hm_s[cur], wup_ref[:, c:c + COLS]), 0.0)
        act_s[:, c:c + COLS] = (up * up).astype(BF16)
        if pieces:
            pieces.pop(0)()
    for piece in pieces:
        piece()
    for c in range(0, D_MODEL, COLS):
        cs = slice(c, c + COLS)
        x1_s[cur, :, cs] = x1_s[cur, :, cs] + _dot(act_s[...], wdown_ref[:, cs])
    if final:
        xo_ref[...] = _rms(x1_s[cur], gf_ref[...])
    else:
        xo_ref[...] = x1_s[cur]


def _memkv_kernel(mem_ref, g_ref, w_ref, mk_ref, mv_ref):
    for b in range(mem_ref.shape[0]):
        h = _rms(mem_ref[b], g_ref[...]).astype(BF16)
        kv = _dot(h, w_ref[...])
        mk_ref[b] = kv[:, 0:MEM_DIM]
        mv_ref[b] = kv[:, MEM_DIM:]


def _params(n_axes):
    return pltpu.CompilerParams(dimension_semantics=("arbitrary",) * n_axes, vmem_limit_bytes=VMEM_LIMIT)


def _resident(shape, layer):
    zeros = (0,) * len(shape)
    return pl.BlockSpec((None,) + tuple(shape), lambda *_: (layer,) + zeros, pipeline_mode=pl.Buffered(1))


def _smem():
    return pl.BlockSpec(memory_space=pltpu.SMEM)


def _split_w_in(win_ref):
    return (win_ref.at[:, 0:IN_CONV], win_ref.at[:, IN_CONV:IN_GATE], win_ref.at[:, IN_GATE:IN_COLS])


def _mixer_weight_specs(layer):
    return [
        _resident((D_MODEL, IN_COLS), layer),
        _resident((CONV_TAPS, D_MODEL), layer),
        _resident((D_MODEL, D_MODEL), layer),
        _resident((D_MODEL, D_MODEL), layer),
        _resident((D_MODEL, D_MODEL), layer),
    ]


def _mixer_weights(w):
    return (w["w_in"], w["conv_w"], w["w_attn_out"], w["w_conv_out"], w["w_mix_out"])


def _mixer_prompt(layer, x, slopes, w):
    batch, seq, _ = x.shape
    tile = PROMPT_TILE
    row = lambda b, t: (b, t, 0)
    next_row = lambda b, t: (b, jnp.minimum(t + 1, seq // tile - 1), 0)
    per_seq = lambda b, t: (b, 0, 0)
    return pl.pallas_call(
        functools.partial(_mixer_prompt_kernel, layer=layer, tile=tile),
        grid=(batch, seq // tile),
        in_specs=[_smem(), _smem(),
                  pl.BlockSpec((None, tile, D_MODEL), row),
                  pl.BlockSpec((None, tile, D_MODEL), next_row),
                  _resident((1, D_MODEL), layer)] + _mixer_weight_specs(layer),
        out_specs=[pl.BlockSpec((None, tile, D_MODEL), row),
                   pl.BlockSpec((None, WINDOW, KV_DIM), per_seq),
                   pl.BlockSpec((None, WINDOW, KV_DIM), per_seq),
                   pl.BlockSpec((1, CONV_LEFT, D_MODEL), per_seq)],
        out_shape=[jax.ShapeDtypeStruct(x.shape, F32),
                   jax.ShapeDtypeStruct((batch, WINDOW, KV_DIM), F32),
                   jax.ShapeDtypeStruct((batch, WINDOW, KV_DIM), F32),
                   jax.ShapeDtypeStruct((batch, CONV_LEFT, D_MODEL), F32)],
        scratch_shapes=[
            pltpu.VMEM((2, tile, D_MODEL), BF16),
            pltpu.VMEM((tile, D_MODEL), BF16),
            pltpu.VMEM((N_KV_HEADS, WINDOW + tile, 2 * LANES), BF16),
            pltpu.VMEM((N_KV_HEADS, HEAD_DIM, WINDOW + tile), BF16),
            pltpu.VMEM((tile, D_MODEL), BF16),
            pltpu.VMEM((1, SUBLANES + tile, D_MODEL), F32),
            pltpu.VMEM((tile, D_MODEL), BF16),
            pltpu.VMEM((2, tile, D_MODEL), F32),
            pltpu.VMEM((tile, D_MODEL), BF16),
            pltpu.VMEM((START_VARIANTS, N_KV_HEADS * BAND_KEYS, 2 * LANES), F32),
            pltpu.VMEM((2, N_KV_HEADS, WINDOW, 2 * LANES), BF16),
            pltpu.VMEM((2, N_KV_HEADS, HEAD_DIM, WINDOW), BF16),
            pltpu.VMEM((2, 1, SUBLANES, D_MODEL), F32),
        ],
        compiler_params=_params(2),
        name=f"mixer_prompt_l{layer}",
    )(slopes, w["sink"], x, x, w["g_mix"], *_mixer_weights(w))


def _mixer_sample(layer, x2, slopes, kc, vc, st, w, rows):
    total = x2.shape[0]
    seqs = SAMPLE_SEQS
    tile = seqs * rows
    keys = WINDOW + rows
    row = lambda i: (i, 0)
    per_seq = lambda i: (layer, i, 0, 0)
    return pl.pallas_call(
        functools.partial(_mixer_sample_kernel, layer=layer, seqs=seqs, rows=rows),
        grid=(total // tile,),
        in_specs=[_smem(), _smem(),
                  pl.BlockSpec((tile, D_MODEL), row),
                  _resident((1, D_MODEL), layer),
                  pl.BlockSpec((None, seqs, WINDOW, KV_DIM), per_seq),
                  pl.BlockSpec((None, seqs, WINDOW, KV_DIM), per_seq),
                  pl.BlockSpec((None, seqs, CONV_LEFT, D_MODEL), per_seq),
                  ] + _mixer_weight_specs(layer),
        out_specs=[pl.BlockSpec((tile, D_MODEL), row),
                   pl.BlockSpec((tile, KV_DIM), row),
                   pl.BlockSpec((tile, KV_DIM), row),
                   pl.BlockSpec((seqs, CONV_LEFT, D_MODEL), lambda i: (i, 0, 0))],
        out_shape=[jax.ShapeDtypeStruct(x2.shape, F32),
                   jax.ShapeDtypeStruct((total, KV_DIM), F32),
                   jax.ShapeDtypeStruct((total, KV_DIM), F32),
                   jax.ShapeDtypeStruct((total // rows, CONV_LEFT, D_MODEL), F32)],
        scratch_shapes=[
            pltpu.VMEM((tile, D_MODEL), BF16),
            pltpu.VMEM((tile, D_MODEL), BF16),
            pltpu.VMEM((N_KV_HEADS, seqs, keys, 2 * LANES), BF16),
            pltpu.VMEM((N_KV_HEADS, seqs, keys, 2 * LANES), BF16),
            pltpu.VMEM((tile, D_MODEL), BF16),
            pltpu.VMEM((seqs, SUBLANES + rows, D_MODEL), F32),
            pltpu.VMEM((tile, D_MODEL), BF16),
            pltpu.VMEM((2, tile, D_MODEL), F32),
            pltpu.VMEM((tile, D_MODEL), BF16),
            pltpu.VMEM((1, N_HEADS * rows, SCORE_COLS), F32),
        ],
        compiler_params=_params(1),
        name=f"mixer_sample_l{layer}",
    )(slopes, w["sink"], x2, w["g_mix"], kc, vc, st, *_mixer_weights(w))


def _ffn(layer, x2, mk, mv, w, groups, rows, final, name):
    total = x2.shape[0]
    tile = groups * rows
    steps_per_group = (total // mk.shape[1]) // rows if groups == 1 else 1
    n_tiles = total // tile
    tile_in = lambda i: jnp.minimum(i, n_tiles - 1)
    mem_spec = pl.BlockSpec((None, groups, N_MEM, MEM_DIM),
                            lambda i: (layer, tile_in(i) // steps_per_group, 0, 0))
    return pl.pallas_call(
        functools.partial(_ffn_kernel, groups=groups, rows=rows, final=final),
        grid=(n_tiles + 1,),
        in_specs=[pl.BlockSpec((tile, D_MODEL), lambda i: (tile_in(i), 0)), mem_spec, mem_spec,
                  _resident((1, D_MODEL), layer),
                  _resident((D_MODEL, MEM_DIM), layer),
                  _resident((MEM_DIM, D_MODEL), layer),
                  _resident((1, D_MODEL), layer),
                  _resident((D_MODEL, D_FF), layer),
                  _resident((D_FF, D_MODEL), layer),
                  pl.BlockSpec((1, D_MODEL), lambda i: (0, 0))],
        out_specs=pl.BlockSpec((tile, D_MODEL), lambda i: (jnp.maximum(i - 1, 0), 0)),
        out_shape=jax.ShapeDtypeStruct(x2.shape, F32),
        scratch_shapes=[
            pltpu.VMEM((tile, D_MODEL), BF16),
            pltpu.VMEM((tile, MEM_DIM), BF16),
            pltpu.VMEM((tile, MEM_DIM), BF16),
            pltpu.VMEM((2, tile, D_MODEL), F32),
            pltpu.VMEM((2, tile, D_MODEL), BF16),
            pltpu.VMEM((tile, D_FF), BF16),
        ],
        compiler_params=_params(1),
        name=name,
    )(x2, mk, mv, w["g_cross"], w["w_cq"], w["w_co"], w["g_mlp"], w["w_up"], w["w_down"], w["g_final"])


def _memory_kv(mem, g_mem, w_ckv):
    batch = mem.shape[0]
    out = jax.ShapeDtypeStruct((DEPTH, batch, N_MEM, MEM_DIM), F32)
    return pl.pallas_call(
        _memkv_kernel,
        grid=(DEPTH,),
        in_specs=[pl.BlockSpec((batch, N_MEM, D_MODEL), lambda l: (0, 0, 0)),
                  pl.BlockSpec((None, 1, D_MODEL), lambda l: (l, 0, 0)),
                  pl.BlockSpec((None, D_MODEL, 2 * MEM_DIM), lambda l: (l, 0, 0))],
        out_specs=[pl.BlockSpec((None, batch, N_MEM, MEM_DIM), lambda l: (l, 0, 0, 0))] * 2,
        out_shape=[out, out],
        compiler_params=_params(1),
        name="memory_kv",
    )(mem, g_mem, w_ckv)


def kernel(x_prompt, x_sample, mem_prompt, cache_attn_k, cache_attn_v, state_conv, cache_mem_k, cache_mem_v,
           norm_mix_g, w_in, conv_w, attn_sink, w_attn_out, w_conv_out, w_mix_out, norm_cross_g, norm_mem_g,
           w_cq, w_ckv, w_co, norm_mlp_g, w_up, w_down, norm_final_g):
    batch, seq, _ = x_prompt.shape
    dec_batch, dec_seq, _ = x_sample.shape
    assert seq % PROMPT_TILE == 0 and PROMPT_TILE % LANES == 0 and PROMPT_TILE >= WINDOW
    assert seq % FFN_TILE == 0
    assert dec_seq <= CHUNK and dec_seq % 16 == 0 and dec_batch % SAMPLE_SEQS == 0
    assert cache_attn_k.shape[2] == WINDOW

    w = {
        "sink": attn_sink,
        "g_mix": norm_mix_g[:, None, :],
        "w_in": w_in.astype(BF16),
        "conv_w": conv_w,
        "w_attn_out": w_attn_out.astype(BF16),
        "w_conv_out": w_conv_out.astype(BF16),
        "w_mix_out": w_mix_out.astype(BF16),
        "g_cross": norm_cross_g[:, None, :],
        "w_cq": w_cq.astype(BF16),
        "w_co": w_co.astype(BF16),
        "g_mlp": norm_mlp_g[:, None, :],
        "w_up": w_up.astype(BF16),
        "w_down": w_down.astype(BF16),
        "g_final": norm_final_g[None, :],
    }
    slopes = jnp.exp2(-8.0 * jnp.arange(1, N_HEADS + 1, dtype=F32) / N_HEADS)

    mk_p, mv_p = _memory_kv(mem_prompt, norm_mem_g[:, None, :], w_ckv.astype(BF16))

    kc = cache_attn_k.reshape(DEPTH, dec_batch, WINDOW, KV_DIM)
    vc = cache_attn_v.reshape(DEPTH, dec_batch, WINDOW, KV_DIM)
    mk_s = cache_mem_k.reshape(DEPTH, dec_batch, N_MEM, MEM_DIM)
    mv_s = cache_mem_v.reshape(DEPTH, dec_batch, N_MEM, MEM_DIM)

    xp = x_prompt
    xs = x_sample.reshape(dec_batch * dec_seq, D_MODEL)
    kp_l, vp_l, cp_l, ks_l, vs_l, cs_l = [], [], [], [], [], []
    for layer in range(DEPTH):
        final = layer == DEPTH - 1
        xp, kp, vp, cp = _mixer_prompt(layer, xp, slopes, w)
        xp = _ffn(layer, xp.reshape(batch * seq, D_MODEL), mk_p, mv_p, w, 1, FFN_TILE, final,
                  f"ffn_prompt_l{layer}").reshape(batch, seq, D_MODEL)
        xs, kn, vn, cs = _mixer_sample(layer, xs, slopes, kc, vc, state_conv, w, dec_seq)
        xs = _ffn(layer, xs, mk_s, mv_s, w, SAMPLE_SEQS, dec_seq, final, f"ffn_sample_l{layer}")
        kp_l.append(kp)
        vp_l.append(vp)
        cp_l.append(cp)
        ks_l.append(kn)
        vs_l.append(vn)
        cs_l.append(cs)

    kv_p = (DEPTH, batch, WINDOW, N_KV_HEADS, HEAD_DIM)
    kv_s = (DEPTH, dec_batch, dec_seq, N_KV_HEADS, HEAD_DIM)
    mem_shape = (DEPTH, batch, N_MEM, MEM_HEADS, MEM_DIM // MEM_HEADS)
    return (xp, xs.reshape(dec_batch, dec_seq, D_MODEL),
            jnp.stack(kp_l).reshape(kv_p), jnp.stack(vp_l).reshape(kv_p), jnp.stack(cp_l),
            mk_p.reshape(mem_shape), mv_p.reshape(mem_shape),
            jnp.stack(ks_l).reshape(kv_s), jnp.stack(vs_l).reshape(kv_s), jnp.stack(cs_l))
```

```python
import functools

import jax
import jax.numpy as jnp
from jax import lax
from jax.experimental import pallas as pl
from jax.experimental.pallas import tpu as pltpu

D_MODEL = 1024
DEPTH = 4
CHUNK = 64
WINDOW = 128
WIN_CHUNKS = WINDOW // CHUNK
HEAD_DIM = 64
N_HEADS = 16
N_KV_HEADS = 4
GROUP = N_HEADS // N_KV_HEADS
KV_DIM = N_KV_HEADS * HEAD_DIM
N_MEM = 256
MEM_HEADS = 4
MEM_DIM = 256
D_FF = 4 * D_MODEL
IN_CONV = N_HEADS * HEAD_DIM + 2 * KV_DIM
IN_GATE = IN_CONV + 3 * D_MODEL
IN_COLS = IN_GATE + 2 * D_MODEL
EPS = 1e-6
ATTN_SCALE = HEAD_DIM ** -0.5
MEM_SCALE = (MEM_DIM // MEM_HEADS) ** -0.5
CONV_TAPS = 3
CONV_LEFT = CONV_TAPS - 1

LANES = 128
SUBLANES = 8
HALF = LANES // 2
SCORE_COLS = 2 * LANES
MASKED = -1e30
V7X_VMEM_BYTES = 64 * 1024 * 1024
VMEM_LIMIT = V7X_VMEM_BYTES - 8 * 1024 * 1024

PROMPT_TILE = 512
FFN_TILE = 512
SAMPLE_SEQS = 8
COLS = 256
FFN_COLS = 512

BF16 = jnp.bfloat16
F32 = jnp.float32


def _dot(a, b):
    return jnp.dot(a, b, preferred_element_type=F32)


def _dot_nt(a, b):
    return lax.dot_general(a, b, (((1,), (1,)), ((), ())), preferred_element_type=F32)


def _rms(x, g):
    return x * lax.rsqrt(jnp.mean(x * x, axis=-1, keepdims=True) + EPS) * g


def _low_half(shape):
    return lax.broadcasted_iota(jnp.int32, shape, len(shape) - 1) < HALF


def _dup_halves(col):
    swapped = pltpu.roll(col, HALF, axis=1)
    lo = _low_half(col.shape)
    return jnp.where(lo, col, swapped), jnp.where(lo, swapped, col)


def _store_lane_tiled(dst_ref, kv, store):
    for c in range(KV_DIM // LANES):
        even, odd = _dup_halves(kv[:, c * LANES:(c + 1) * LANES])
        store(dst_ref, 2 * c, even)
        store(dst_ref, 2 * c + 1, odd)


def _pad_keys(block):
    pad = jnp.zeros((SCORE_COLS - block.shape[0], block.shape[1]), block.dtype)
    return jnp.concatenate([block, pad], axis=0)


def _head_rows(qa, qb):
    lo = _low_half(qa.shape)
    zero = jnp.zeros_like(qa)
    blocks = (
        jnp.concatenate([jnp.where(lo, qa, zero), zero], axis=1),
        jnp.concatenate([jnp.where(lo, zero, qa), zero], axis=1),
        jnp.concatenate([zero, jnp.where(lo, qb, zero)], axis=1),
        jnp.concatenate([zero, jnp.where(lo, zero, qb)], axis=1),
    )
    return jnp.concatenate(blocks, axis=0)


def _pick_heads(o4, rows):
    lo = _low_half((rows, LANES))
    oa = jnp.where(lo, o4[0:rows, 0:LANES], o4[rows:2 * rows, 0:LANES])
    ob = jnp.where(lo, o4[2 * rows:3 * rows, LANES:], o4[3 * rows:4 * rows, LANES:])
    return oa, ob


def _build_score_bias(bias_ref, slopes_ref, sink_ref, layer, nq, nk, first_valid_keys):
    qi = lax.broadcasted_iota(jnp.int32, (nq, SCORE_COLS), 0)
    ji = lax.broadcasted_iota(jnp.int32, (nq, SCORE_COLS), 1)
    dist = jnp.abs(qi + WINDOW - ji).astype(F32)
    for h in range(N_HEADS):
        base = jnp.where(ji == nk, sink_ref[layer, h], -slopes_ref[h] * dist)
        for v, first in enumerate(first_valid_keys):
            valid = (ji >= first) & (ji <= nk)
            bias_ref[v, h * nq:(h + 1) * nq, :] = jnp.where(valid, base, MASKED)


def _softmax(s):
    p = jnp.exp(s - jnp.max(s, axis=-1, keepdims=True))
    return p * (1.0 / jnp.sum(p, axis=-1, keepdims=True))


def _group_attention_unit(q_s, a_s, r0, nq, kv_head, kblock_of, vblock_of, bias_of):
    c0 = kv_head * GROUP * HEAD_DIM

    def scores():
        qs = _head_rows(q_s[r0:r0 + nq, c0:c0 + LANES], q_s[r0:r0 + nq, c0 + LANES:c0 + 2 * LANES])
        return _dot_nt(qs, _pad_keys(kblock_of())) + bias_of()

    def values(p):
        oa, ob = _pick_heads(_dot(p.astype(BF16), _pad_keys(vblock_of())), nq)
        a_s[r0:r0 + nq, c0:c0 + LANES] = oa.astype(BF16)
        a_s[r0:r0 + nq, c0 + LANES:c0 + 2 * LANES] = ob.astype(BF16)

    return scores, _softmax, values


def _emit_pipelined(units, side):
    slots = len(units) + 2
    total = sum(cost for cost, _ in side)
    emitted = 0
    state = {}
    for j in range(slots):
        if j < len(units):
            state[j] = units[j][0]()
        if 0 <= j - 1 < len(units):
            state[j - 1] = units[j - 1][1](state[j - 1])
        while side and emitted * slots < (j + 1) * total:
            cost, emit = side.pop(0)
            emit()
            emitted += cost
        if 0 <= j - 2 < len(units):
            units[j - 2][2](state.pop(j - 2))
    for _, emit in side:
        emit()


def _q_chunk(c, h_s, q_s, wqkv_ref, scale):
    q_s[:, c:c + COLS] = (_dot(h_s[...], wqkv_ref[:, c:c + COLS]) * scale).astype(BF16)


def _project_kv(h_s, wqkv_ref):
    hb = h_s[...]
    k = _dot(hb, wqkv_ref[:, D_MODEL:D_MODEL + KV_DIM])
    v = _dot(hb, wqkv_ref[:, D_MODEL + KV_DIM:D_MODEL + 2 * KV_DIM])
    return k, v


def _gate_chunk(c, h_s, gate_s, wgate_ref):
    hb = h_s[...]
    gate_s[0, :, c:c + COLS] = jax.nn.sigmoid(_dot(hb, wgate_ref[:, c:c + COLS]))
    gate_s[1, :, c:c + COLS] = jax.nn.sigmoid(_dot(hb, wgate_ref[:, D_MODEL + c:D_MODEL + c + COLS]))


def _conv_chunk(c, h_s, boin_s, u3, rows, co_ref, wconv_ref, convw_ref, next_left_ref=None):
    groups = u3.shape[0]
    total = groups * rows
    cs = slice(c, c + COLS)
    hb = h_s[...]
    ch = _dot(hb, wconv_ref[:, c:c + COLS])
    cc = _dot(hb, wconv_ref[:, 2 * D_MODEL + c:2 * D_MODEL + c + COLS])
    u = cc * ch
    u3[:, SUBLANES:SUBLANES + rows, cs] = u.reshape(groups, rows, COLS)
    um2 = u3[:, SUBLANES - 2:SUBLANES - 2 + rows, cs].reshape(total, COLS)
    um1 = u3[:, SUBLANES - 1:SUBLANES - 1 + rows, cs].reshape(total, COLS)
    conv = um2 * convw_ref[0:1, cs] + um1 * convw_ref[1:2, cs] + u * convw_ref[2:3, cs]
    cb = _dot(hb, wconv_ref[:, D_MODEL + c:D_MODEL + c + COLS])
    boin_s[:, cs] = (cb * conv).astype(BF16)
    tail = u3[:, SUBLANES + rows - CONV_LEFT:SUBLANES + rows, cs]
    co_ref[:, :, cs] = tail
    if next_left_ref is not None:
        next_left_ref[:, SUBLANES - CONV_LEFT:SUBLANES, cs] = tail


def _merge(a_s, boin_s, gate_s, mix_s, x_ref, xo_ref, wao_ref, wco_ref, wmo_ref, anchor=None):
    for c in range(0, D_MODEL, COLS):
        cs = slice(c, c + COLS)
        a = _dot(a_s[...], wao_ref[:, cs])
        bo = _dot(boin_s[...], wco_ref[:, cs])
        mixed = (gate_s[0, :, cs] * a + gate_s[1, :, cs] * bo).astype(BF16)
        if anchor is not None:
            never, value = anchor
            mixed = jnp.where(never, value[:, cs], mixed)
        mix_s[:, cs] = mixed
    for c in range(0, D_MODEL, COLS):
        cs = slice(c, c + COLS)
        xo_ref[:, cs] = x_ref[:, cs] + _dot(mix_s[...], wmo_ref[:, cs])


BAND_HEAD_ORDER = (0, 2, 1, 3)
BAND_KEYS = WINDOW + CHUNK
START_VARIANTS = WIN_CHUNKS + 1
BLOCKS_PER_STAGE = 2
LOG2E = 1.4426950408889634
BAND_Q_SCALE = ATTN_SCALE * LOG2E


def _build_band_bias(bias_ref, slopes_ref):
    ji = lax.broadcasted_iota(jnp.int32, (BAND_KEYS, 2 * LANES), 0)
    li = lax.broadcasted_iota(jnp.int32, (BAND_KEYS, 2 * LANES), 1)
    dist = jnp.abs(li % CHUNK + WINDOW - ji).astype(F32)
    slot = li // CHUNK
    for kh in range(N_KV_HEADS):
        slope = jnp.zeros((BAND_KEYS, 2 * LANES), F32)
        for s, g in enumerate(BAND_HEAD_ORDER):
            slope = jnp.where(slot == s, slopes_ref[kh * GROUP + g], slope)
        base = -slope * dist * LOG2E
        rows = slice(kh * BAND_KEYS, (kh + 1) * BAND_KEYS)
        for v in range(START_VARIANTS):
            bias_ref[v, rows, :] = jnp.where(ji >= (WIN_CHUNKS - v) * CHUNK, base, MASKED)


def _band_query_rows(qa, qb):
    lo = _low_half(qa.shape)
    zero = jnp.zeros_like(qa)
    blocks = (
        jnp.concatenate([jnp.where(lo, qa, zero), zero], axis=1),
        jnp.concatenate([zero, jnp.where(lo, qb, zero)], axis=1),
        jnp.concatenate([jnp.where(lo, zero, qa), zero], axis=1),
        jnp.concatenate([zero, jnp.where(lo, zero, qb)], axis=1),
    )
    return jnp.concatenate(blocks, axis=0)


def _band_scores(q_s, r0, kv_head, kblock, bias):
    c0 = kv_head * GROUP * HEAD_DIM
    qs = _band_query_rows(q_s[r0:r0 + CHUNK, c0:c0 + LANES], q_s[r0:r0 + CHUNK, c0 + LANES:c0 + 2 * LANES])
    return _dot_nt(kblock, qs) + bias


def _band_softmax(st, kv_head, sink_of):
    lo = _low_half((1, LANES))
    probs, inv_den = [], []
    for col in range(2):
        s = st[:, col * LANES:(col + 1) * LANES]
        ha, hb = BAND_HEAD_ORDER[2 * col], BAND_HEAD_ORDER[2 * col + 1]
        sink = jnp.where(lo, sink_of(kv_head * GROUP + ha), sink_of(kv_head * GROUP + hb)) * LOG2E
        m = jnp.maximum(jnp.max(s, axis=0, keepdims=True), sink)
        p = jnp.exp2(s - m)
        inv_den.append(1.0 / (jnp.sum(p, axis=0, keepdims=True) + jnp.exp2(sink - m)))
        probs.append(p.astype(BF16))
    return jnp.concatenate(probs, axis=1), inv_den


def _band_values(vt_block, lead, pt):
    parts = [pt]
    if lead:
        parts.insert(0, jnp.zeros((lead, 2 * LANES), BF16))
    if vt_block.shape[1] > lead + BAND_KEYS:
        parts.append(jnp.zeros((vt_block.shape[1] - lead - BAND_KEYS, 2 * LANES), BF16))
    return _dot(vt_block, jnp.concatenate(parts, axis=0))


def _band_store(a_s, r0, kv_head, ot, inv_den):
    c0 = kv_head * GROUP * HEAD_DIM
    stacked = jnp.concatenate([ot[:, 0:LANES] * inv_den[0], ot[:, LANES:] * inv_den[1]], axis=0)
    out = stacked.T.astype(BF16)
    a_s[r0:r0 + CHUNK, c0:c0 + LANES] = out[0:CHUNK]
    a_s[r0:r0 + CHUNK, c0 + LANES:c0 + 2 * LANES] = out[CHUNK:]


def _mixer_prompt_kernel(slopes_ref, sink_ref, x_ref, xn_ref, g_ref, win_ref, convw_ref,
                         wao_ref, wco_ref, wmo_ref,
                         xo_ref, ko_ref, vo_ref, co_ref,
                         h2_s, q_s, k4_s, vt_s, a_s, u_s, boin_s, gate_s, mix_s, bias_s,
                         kprev_s, vtprev_s, uprev_s, *, layer, tile):
    wqkv_ref, wconv_ref, wgate_ref = _split_w_in(win_ref)
    t = pl.program_id(1)
    cur = t % 2
    nxt = 1 - cur

    @pl.when(t == 0)
    def _():
        _build_band_bias(bias_s, slopes_ref)
        kprev_s[cur] = jnp.zeros(kprev_s.shape[1:], BF16)
        vtprev_s[cur] = jnp.zeros(vtprev_s.shape[1:], BF16)
        uprev_s[cur] = jnp.zeros(uprev_s.shape[1:], F32)
        h2_s[cur] = _rms(x_ref[...], g_ref[...]).astype(BF16)

    h_s = h2_s.at[cur]


    k4_s[:, 0:WINDOW, :] = kprev_s[cur]
    vt_s[:, :, 0:WINDOW] = vtprev_s[cur]
    left = slice(SUBLANES - CONV_LEFT, SUBLANES)
    u_s[:, left, :] = uprev_s[cur, :, left, :]

    cols = range(0, D_MODEL, COLS)
    side = [(3 * COLS, functools.partial(_conv_chunk, c, h_s, boin_s, u_s, tile, co_ref, wconv_ref, convw_ref,
                                         uprev_s.at[nxt])) for c in cols]
    side += [(2 * COLS, functools.partial(_gate_chunk, c, h_s, gate_s, wgate_ref)) for c in cols]

    k, v = _project_kv(h_s, wqkv_ref)
    for c in cols:
        _q_chunk(c, h_s, q_s, wqkv_ref, BAND_Q_SCALE)
    side.pop(0)[1]()
    ko_ref[...] = k[tile - WINDOW:, :]
    vo_ref[...] = v[tile - WINDOW:, :]

    def store_new(dst_ref, head, dup):
        dup = dup.astype(BF16)
        dst_ref[head, WINDOW:WINDOW + tile, 0:LANES] = dup
        dst_ref[head, WINDOW:WINDOW + tile, LANES:] = dup

    _store_lane_tiled(k4_s, k, store_new)
    vt = v.T.astype(BF16)
    for kh in range(N_KV_HEADS):
        vt_s[kh, :, WINDOW:WINDOW + tile] = vt[kh * HEAD_DIM:(kh + 1) * HEAD_DIM, :]

    blocks = [(kh, r0) for kh in range(N_KV_HEADS) for r0 in range(0, tile, CHUNK)]
    stages = [blocks[i:i + BLOCKS_PER_STAGE] for i in range(0, len(blocks), BLOCKS_PER_STAGE)]
    side_per_stage = sum(cost for cost, _ in side) / len(stages)

    sink_of = lambda h: sink_ref[layer, h]

    def scores(kh, r0):
        chunk = r0 // CHUNK
        variant = jnp.where(t == 0, chunk, WIN_CHUNKS) if chunk < WIN_CHUNKS else WIN_CHUNKS
        bias = bias_s[variant, kh * BAND_KEYS:(kh + 1) * BAND_KEYS, :]
        return _band_scores(q_s, r0, kh, k4_s[kh, r0:r0 + BAND_KEYS, :], bias)

    def values(kh, r0, pt):
        lead = r0 % LANES
        return _band_values(vt_s[kh, :, r0 - lead:r0 - lead + 2 * LANES], lead, pt)

    emitted = 0
    st_next = [scores(*b) for b in stages[0]]
    for i, stage in enumerate(stages):
        st = st_next
        if i + 1 < len(stages):
            st_next = [scores(*b) for b in stages[i + 1]]
        soft = [_band_softmax(s, kh, sink_of) for s, (kh, _) in zip(st, stage)]
        while side and emitted < (i + 1) * side_per_stage:
            cost, emit = side.pop(0)
            emit()
            emitted += cost
        outs = [values(kh, r0, pt) for (pt, _), (kh, r0) in zip(soft, stage)]
        for ot, (_, inv_den), (kh, r0) in zip(outs, soft, stage):
            _band_store(a_s, r0, kh, ot, inv_den)
    for _, emit in side:
        emit()

    kprev_s[nxt] = k4_s[:, tile:tile + WINDOW, :]
    vtprev_s[nxt] = vt_s[:, :, tile:tile + WINDOW]

    h_next = _rms(xn_ref[...], g_ref[...]).astype(BF16)
    h2_s[nxt] = h_next
    _merge(a_s, boin_s, gate_s, mix_s, x_ref, xo_ref, wao_ref, wco_ref, wmo_ref, anchor=(t < 0, h_next))


def _mixer_sample_kernel(slopes_ref, sink_ref, x_ref, g_ref, kc_ref, vc_ref, st_ref,
                         win_ref, convw_ref, wao_ref, wco_ref, wmo_ref,
                         xo_ref, ko_ref, vo_ref, co_ref,
                         h_s, q_s, k4_s, v4_s, a_s, u_s, boin_s, gate_s, mix_s, bias_s, *, layer, seqs, rows):
    wqkv_ref, wconv_ref, wgate_ref = _split_w_in(win_ref)
    keys = WINDOW + rows

    @pl.when(pl.program_id(0) == 0)
    def _():
        _build_score_bias(bias_s, slopes_ref, sink_ref, layer, rows, keys, (0,))

    u_s[:, SUBLANES - CONV_LEFT:SUBLANES, :] = st_ref[...]

    h_s[...] = _rms(x_ref[...], g_ref[...]).astype(BF16)
    for c in range(0, D_MODEL, COLS):
        _q_chunk(c, h_s, q_s, wqkv_ref, ATTN_SCALE)
    k, v = _project_kv(h_s, wqkv_ref)
    ko_ref[...] = k
    vo_ref[...] = v

    def store_cache(dst_ref, head, dup):
        dup = dup.reshape(seqs, WINDOW, LANES).astype(BF16)
        dst_ref[head, :, 0:WINDOW, 0:LANES] = dup
        dst_ref[head, :, 0:WINDOW, LANES:] = dup

    def store_new(dst_ref, head, dup):
        dup = dup.reshape(seqs, rows, LANES).astype(BF16)
        dst_ref[head, :, WINDOW:keys, 0:LANES] = dup
        dst_ref[head, :, WINDOW:keys, LANES:] = dup

    _store_lane_tiled(k4_s, kc_ref[...].reshape(seqs * WINDOW, KV_DIM), store_cache)
    _store_lane_tiled(v4_s, vc_ref[...].reshape(seqs * WINDOW, KV_DIM), store_cache)
    _store_lane_tiled(k4_s, k, store_new)
    _store_lane_tiled(v4_s, v, store_new)

    units = [
        _group_attention_unit(
            q_s, a_s, g * rows, rows, kh,
            functools.partial(lambda kh, g: k4_s[kh, g], kh, g),
            functools.partial(lambda kh, g: v4_s[kh, g], kh, g),
            functools.partial(lambda kh: bias_s[0, kh * GROUP * rows:(kh + 1) * GROUP * rows, :], kh))
        for kh in range(N_KV_HEADS) for g in range(seqs)]
    side = [(3 * COLS, functools.partial(_conv_chunk, c, h_s, boin_s, u_s, rows, co_ref, wconv_ref, convw_ref))
            for c in range(0, D_MODEL, COLS)]
    side += [(2 * COLS, functools.partial(_gate_chunk, c, h_s, gate_s, wgate_ref)) for c in range(0, D_MODEL, COLS)]
    _emit_pipelined(units, side)
    _merge(a_s, boin_s, gate_s, mix_s, x_ref, xo_ref, wao_ref, wco_ref, wmo_ref)


def _cross_attention_units(cq_s, co_s, mk_ref, mv_ref, groups, rows):
    lo = _low_half((rows, LANES))
    zero = jnp.zeros((rows, LANES), BF16)
    stacked = MEM_HEADS * rows <= 2 * LANES
    units = []
    for g in range(groups):
        rs = slice(g * rows, (g + 1) * rows)
        if stacked:
            def scores(rs=rs, g=g):
                return _dot_nt(_head_rows(cq_s[rs, 0:LANES], cq_s[rs, LANES:]), mk_ref[g].astype(BF16))

            def values(p, rs=rs, g=g):
                oa, ob = _pick_heads(_dot(p.astype(BF16), mv_ref[g].astype(BF16)), rows)
                co_s[rs, 0:LANES] = oa.astype(BF16)
                co_s[rs, LANES:] = ob.astype(BF16)

            units.append((scores, _softmax, values))
            continue
        held = {}
        for hh in range(MEM_HEADS):
            c, odd = divmod(hh, 2)

            def scores(rs=rs, g=g, c=c, odd=odd):
                qcol = cq_s[rs, c * LANES:(c + 1) * LANES]
                qm = jnp.where(lo, zero, qcol) if odd else jnp.where(lo, qcol, zero)
                qm = jnp.concatenate([qm, zero] if c == 0 else [zero, qm], axis=1)
                return _dot_nt(qm, mk_ref[g].astype(BF16))

            def values(p, rs=rs, g=g, c=c, odd=odd, held=held):
                out = _dot(p.astype(BF16), mv_ref[g].astype(BF16))[:, c * LANES:(c + 1) * LANES]
                if odd:
                    co_s[rs, c * LANES:(c + 1) * LANES] = jnp.where(lo, held.pop(c), out).astype(BF16)
                else:
                    held[c] = out

            units.append((scores, _softmax, values))
    return units


def _ffn_kernel(x_ref, mk_ref, mv_ref, gc_ref, wcq_ref, wcout_ref, gm_ref, wup_ref, wdown_ref, gf_ref,
                xo_ref,
                h_s, cq_s, co_s, x1_s, hm_s, act_s, *, groups, rows, final):
    i = pl.program_id(0)
    cur = i % 2
    nxt = 1 - cur

    @pl.when(i == 0)
    def _():
        x1_s[cur] = jnp.zeros(x1_s.shape[1:], F32)
        hm_s[cur] = jnp.zeros(hm_s.shape[1:], BF16)

    def norm_in():
        h_s[...] = _rms(x_ref[...], gc_ref[...]).astype(BF16)

    def project_q():
        cq_s[...] = (_dot(h_s[...], wcq_ref[...]) * MEM_SCALE).astype(BF16)

    def residual(c):
        cs = slice(c, c + FFN_COLS)
        x1_s[nxt, :, cs] = x_ref[:, cs] + _dot(co_s[...], wcout_ref[:, cs])

    def norm_mlp():
        hm_s[nxt] = _rms(x1_s[nxt], gm_ref[...]).astype(BF16)

    units = _cross_attention_units(cq_s, co_s, mk_ref, mv_ref, groups, rows)
    pieces = [norm_in, project_q]
    state = {}
    for j in range(len(units) + 2):
        def piece(j=j):
            if j < len(units):
                state[j] = units[j][0]()
            if 0 <= j - 1 < len(units):
                state[j - 1] = units[j - 1][1](state[j - 1])
            if 0 <= j - 2 < len(units):
                units[j - 2][2](state.pop(j - 2))
        pieces.append(piece)
    pieces += [functools.partial(residual, c) for c in range(0, D_MODEL, FFN_COLS)]
    pieces.append(norm_mlp)

    for c in range(0, D_FF, FFN_COLS):
        up = jnp.maximum(_dot(hm_s[cur], wup_ref[:, c:c + FFN_COLS]), 0.0)
        act_s[:, c:c + FFN_COLS] = (up * up).astype(BF16)
        if pieces:
            pieces.pop(0)()
    for piece in pieces:
        piece()
    for c in range(0, D_MODEL, FFN_COLS):
        cs = slice(c, c + FFN_COLS)
        x1_s[cur, :, cs] = x1_s[cur, :, cs] + _dot(act_s[...], wdown_ref[:, cs])
    if final:
        xo_ref[...] = _rms(x1_s[cur], gf_ref[...])
    else:
        xo_ref[...] = x1_s[cur]


def _memkv_kernel(mem_ref, g_ref, w_ref, mk_ref, mv_ref):
    for b in range(mem_ref.shape[0]):
        h = _rms(mem_ref[b], g_ref[...]).astype(BF16)
        kv = _dot(h, w_ref[...])
        mk_ref[b] = kv[:, 0:MEM_DIM]
        mv_ref[b] = kv[:, MEM_DIM:]


def _params(n_axes):
    return pltpu.CompilerParams(dimension_semantics=("arbitrary",) * n_axes, vmem_limit_bytes=VMEM_LIMIT)


def _resident(shape, layer):
    zeros = (0,) * len(shape)
    return pl.BlockSpec((None,) + tuple(shape), lambda *_: (layer,) + zeros, pipeline_mode=pl.Buffered(1))


def _smem():
    return pl.BlockSpec(memory_space=pltpu.SMEM)


def _split_w_in(win_ref):
    return (win_ref.at[:, 0:IN_CONV], win_ref.at[:, IN_CONV:IN_GATE], win_ref.at[:, IN_GATE:IN_COLS])


def _mixer_weight_specs(layer):
    return [
        _resident((D_MODEL, IN_COLS), layer),
        _resident((CONV_TAPS, D_MODEL), layer),
        _resident((D_MODEL, D_MODEL), layer),
        _resident((D_MODEL, D_MODEL), layer),
        _resident((D_MODEL, D_MODEL), layer),
    ]


def _mixer_weights(w):
    return (w["w_in"], w["conv_w"], w["w_attn_out"], w["w_conv_out"], w["w_mix_out"])


def _mixer_prompt(layer, x, slopes, w):
    batch, seq, _ = x.shape
    tile = PROMPT_TILE
    row = lambda b, t: (b, t, 0)
    next_row = lambda b, t: (b, jnp.minimum(t + 1, seq // tile - 1), 0)
    per_seq = lambda b, t: (b, 0, 0)
    return pl.pallas_call(
        functools.partial(_mixer_prompt_kernel, layer=layer, tile=tile),
        grid=(batch, seq // tile),
        in_specs=[_smem(), _smem(),
                  pl.BlockSpec((None, tile, D_MODEL), row),
                  pl.BlockSpec((None, tile, D_MODEL), next_row),
                  _resident((1, D_MODEL), layer)] + _mixer_weight_specs(layer),
        out_specs=[pl.BlockSpec((None, tile, D_MODEL), row),
                   pl.BlockSpec((None, WINDOW, KV_DIM), per_seq),
                   pl.BlockSpec((None, WINDOW, KV_DIM), per_seq),
                   pl.BlockSpec((1, CONV_LEFT, D_MODEL), per_seq)],
        out_shape=[jax.ShapeDtypeStruct(x.shape, F32),
                   jax.ShapeDtypeStruct((batch, WINDOW, KV_DIM), F32),
                   jax.ShapeDtypeStruct((batch, WINDOW, KV_DIM), F32),
                   jax.ShapeDtypeStruct((batch, CONV_LEFT, D_MODEL), F32)],
        scratch_shapes=[
            pltpu.VMEM((2, tile, D_MODEL), BF16),
            pltpu.VMEM((tile, D_MODEL), BF16),
            pltpu.VMEM((N_KV_HEADS, WINDOW + tile, 2 * LANES), BF16),
            pltpu.VMEM((N_KV_HEADS, HEAD_DIM, WINDOW + tile), BF16),
            pltpu.VMEM((tile, D_MODEL), BF16),
            pltpu.VMEM((1, SUBLANES + tile, D_MODEL), F32),
            pltpu.VMEM((tile, D_MODEL), BF16),
            pltpu.VMEM((2, tile, D_MODEL), F32),
            pltpu.VMEM((tile, D_MODEL), BF16),
            pltpu.VMEM((START_VARIANTS, N_KV_HEADS * BAND_KEYS, 2 * LANES), F32),
            pltpu.VMEM((2, N_KV_HEADS, WINDOW, 2 * LANES), BF16),
            pltpu.VMEM((2, N_KV_HEADS, HEAD_DIM, WINDOW), BF16),
            pltpu.VMEM((2, 1, SUBLANES, D_MODEL), F32),
        ],
        compiler_params=_params(2),
        name=f"mixer_prompt_l{layer}",
    )(slopes, w["sink"], x, x, w["g_mix"], *_mixer_weights(w))


def _mixer_sample(layer, x2, slopes, kc, vc, st, w, rows):
    total = x2.shape[0]
    seqs = SAMPLE_SEQS
    tile = seqs * rows
    keys = WINDOW + rows
    row = lambda i: (i, 0)
    per_seq = lambda i: (layer, i, 0, 0)
    return pl.pallas_call(
        functools.partial(_mixer_sample_kernel, layer=layer, seqs=seqs, rows=rows),
        grid=(total // tile,),
        in_specs=[_smem(), _smem(),
                  pl.BlockSpec((tile, D_MODEL), row),
                  _resident((1, D_MODEL), layer),
                  pl.BlockSpec((None, seqs, WINDOW, KV_DIM), per_seq),
                  pl.BlockSpec((None, seqs, WINDOW, KV_DIM), per_seq),
                  pl.BlockSpec((None, seqs, CONV_LEFT, D_MODEL), per_seq),
                  ] + _mixer_weight_specs(layer),
        out_specs=[pl.BlockSpec((tile, D_MODEL), row),
                   pl.BlockSpec((tile, KV_DIM), row),
                   pl.BlockSpec((tile, KV_DIM), row),
                   pl.BlockSpec((seqs, CONV_LEFT, D_MODEL), lambda i: (i, 0, 0))],
        out_shape=[jax.ShapeDtypeStruct(x2.shape, F32),
                   jax.ShapeDtypeStruct((total, KV_DIM), F32),
                   jax.ShapeDtypeStruct((total, KV_DIM), F32),
                   jax.ShapeDtypeStruct((total // rows, CONV_LEFT, D_MODEL), F32)],
        scratch_shapes=[
            pltpu.VMEM((tile, D_MODEL), BF16),
            pltpu.VMEM((tile, D_MODEL), BF16),
            pltpu.VMEM((N_KV_HEADS, seqs, keys, 2 * LANES), BF16),
            pltpu.VMEM((N_KV_HEADS, seqs, keys, 2 * LANES), BF16),
            pltpu.VMEM((tile, D_MODEL), BF16),
            pltpu.VMEM((seqs, SUBLANES + rows, D_MODEL), F32),
            pltpu.VMEM((tile, D_MODEL), BF16),
            pltpu.VMEM((2, tile, D_MODEL), F32),
            pltpu.VMEM((tile, D_MODEL), BF16),
            pltpu.VMEM((1, N_HEADS * rows, SCORE_COLS), F32),
        ],
        compiler_params=_params(1),
        name=f"mixer_sample_l{layer}",
    )(slopes, w["sink"], x2, w["g_mix"], kc, vc, st, *_mixer_weights(w))


def _ffn(layer, x2, mk, mv, w, groups, rows, final, name):
    total = x2.shape[0]
    tile = groups * rows
    steps_per_group = (total // mk.shape[1]) // rows if groups == 1 else 1
    n_tiles = total // tile
    tile_in = lambda i: jnp.minimum(i, n_tiles - 1)
    mem_spec = pl.BlockSpec((None, groups, N_MEM, MEM_DIM),
                            lambda i: (layer, tile_in(i) // steps_per_group, 0, 0))
    return pl.pallas_call(
        functools.partial(_ffn_kernel, groups=groups, rows=rows, final=final),
        grid=(n_tiles + 1,),
        in_specs=[pl.BlockSpec((tile, D_MODEL), lambda i: (tile_in(i), 0)), mem_spec, mem_spec,
                  _resident((1, D_MODEL), layer),
                  _resident((D_MODEL, MEM_DIM), layer),
                  _resident((MEM_DIM, D_MODEL), layer),
                  _resident((1, D_MODEL), layer),
                  _resident((D_MODEL, D_FF), layer),
                  _resident((D_FF, D_MODEL), layer),
                  pl.BlockSpec((1, D_MODEL), lambda i: (0, 0))],
        out_specs=pl.BlockSpec((tile, D_MODEL), lambda i: (jnp.maximum(i - 1, 0), 0)),
        out_shape=jax.ShapeDtypeStruct(x2.shape, F32),
        scratch_shapes=[
            pltpu.VMEM((tile, D_MODEL), BF16),
            pltpu.VMEM((tile, MEM_DIM), BF16),
            pltpu.VMEM((tile, MEM_DIM), BF16),
            pltpu.VMEM((2, tile, D_MODEL), F32),
            pltpu.VMEM((2, tile, D_MODEL), BF16),
            pltpu.VMEM((tile, D_FF), BF16),
        ],
        compiler_params=_params(1),
        name=name,
    )(x2, mk, mv, w["g_cross"], w["w_cq"], w["w_co"], w["g_mlp"], w["w_up"], w["w_down"], w["g_final"])


def _memory_kv(mem, g_mem, w_ckv):
    batch = mem.shape[0]
    out = jax.ShapeDtypeStruct((DEPTH, batch, N_MEM, MEM_DIM), F32)
    return pl.pallas_call(
        _memkv_kernel,
        grid=(DEPTH,),
        in_specs=[pl.BlockSpec((batch, N_MEM, D_MODEL), lambda l: (0, 0, 0)),
                  pl.BlockSpec((None, 1, D_MODEL), lambda l: (l, 0, 0)),
                  pl.BlockSpec((None, D_MODEL, 2 * MEM_DIM), lambda l: (l, 0, 0))],
        out_specs=[pl.BlockSpec((None, batch, N_MEM, MEM_DIM), lambda l: (l, 0, 0, 0))] * 2,
        out_shape=[out, out],
        compiler_params=_params(1),
        name="memory_kv",
    )(mem, g_mem, w_ckv)


def kernel(x_prompt, x_sample, mem_prompt, cache_attn_k, cache_attn_v, state_conv, cache_mem_k, cache_mem_v,
           norm_mix_g, w_in, conv_w, attn_sink, w_attn_out, w_conv_out, w_mix_out, norm_cross_g, norm_mem_g,
           w_cq, w_ckv, w_co, norm_mlp_g, w_up, w_down, norm_final_g):
    batch, seq, _ = x_prompt.shape
    dec_batch, dec_seq, _ = x_sample.shape
    assert seq % PROMPT_TILE == 0 and PROMPT_TILE % LANES == 0 and PROMPT_TILE >= WINDOW
    assert seq % FFN_TILE == 0
    assert dec_seq <= CHUNK and dec_seq % 16 == 0 and dec_batch % SAMPLE_SEQS == 0
    assert cache_attn_k.shape[2] == WINDOW

    w = {
        "sink": attn_sink,
        "g_mix": norm_mix_g[:, None, :],
        "w_in": w_in.astype(BF16),
        "conv_w": conv_w,
        "w_attn_out": w_attn_out.astype(BF16),
        "w_conv_out": w_conv_out.astype(BF16),
        "w_mix_out": w_mix_out.astype(BF16),
        "g_cross": norm_cross_g[:, None, :],
        "w_cq": w_cq.astype(BF16),
        "w_co": w_co.astype(BF16),
        "g_mlp": norm_mlp_g[:, None, :],
        "w_up": w_up.astype(BF16),
        "w_down": w_down.astype(BF16),
        "g_final": norm_final_g[None, :],
    }
    slopes = jnp.exp2(-8.0 * jnp.arange(1, N_HEADS + 1, dtype=F32) / N_HEADS)

    mk_p, mv_p = _memory_kv(mem_prompt, norm_mem_g[:, None, :], w_ckv.astype(BF16))

    kc = cache_attn_k.reshape(DEPTH, dec_batch, WINDOW, KV_DIM)
    vc = cache_attn_v.reshape(DEPTH, dec_batch, WINDOW, KV_DIM)
    mk_s = cache_mem_k.reshape(DEPTH, dec_batch, N_MEM, MEM_DIM).astype(BF16)
    mv_s = cache_mem_v.reshape(DEPTH, dec_batch, N_MEM, MEM_DIM).astype(BF16)

    xp = x_prompt
    xs = x_sample.reshape(dec_batch * dec_seq, D_MODEL)
    kp_l, vp_l, cp_l, ks_l, vs_l, cs_l = [], [], [], [], [], []
    for layer in range(DEPTH):
        final = layer == DEPTH - 1
        xp, kp, vp, cp = _mixer_prompt(layer, xp, slopes, w)
        xp = _ffn(layer, xp.reshape(batch * seq, D_MODEL), mk_p, mv_p, w, 1, FFN_TILE, final,
                  f"ffn_prompt_l{layer}").reshape(batch, seq, D_MODEL)
        xs, kn, vn, cs = _mixer_sample(layer, xs, slopes, kc, vc, state_conv, w, dec_seq)
        xs = _ffn(layer, xs, mk_s, mv_s, w, SAMPLE_SEQS, dec_seq, final, f"ffn_sample_l{layer}")
        kp_l.append(kp)
        vp_l.append(vp)
        cp_l.append(cp)
        ks_l.append(kn)
        vs_l.append(vn)
        cs_l.append(cs)

    kv_p = (DEPTH, batch, WINDOW, N_KV_HEADS, HEAD_DIM)
    kv_s = (DEPTH, dec_batch, dec_seq, N_KV_HEADS, HEAD_DIM)
    mem_shape = (DEPTH, batch, N_MEM, MEM_HEADS, MEM_DIM // MEM_HEADS)
    return (xp, xs.reshape(dec_batch, dec_seq, D_MODEL),
            jnp.stack(kp_l).reshape(kv_p), jnp.stack(vp_l).reshape(kv_p), jnp.stack(cp_l),
            mk_p.reshape(mem_shape), mv_p.reshape(mem_shape),
            jnp.stack(ks_l).reshape(kv_s), jnp.stack(vs_l).reshape(kv_s), jnp.stack(cs_l))
```

```python
import functools

import jax
import jax.numpy as jnp
from jax import lax
from jax.experimental import pallas as pl
from jax.experimental.pallas import tpu as pltpu

D_MODEL = 1024
DEPTH = 4
CHUNK = 64
WINDOW = 128
WIN_CHUNKS = WINDOW // CHUNK
HEAD_DIM = 64
N_HEADS = 16
N_KV_HEADS = 4
GROUP = N_HEADS // N_KV_HEADS
KV_DIM = N_KV_HEADS * HEAD_DIM
N_MEM = 256
MEM_HEADS = 4
MEM_DIM = 256
D_FF = 4 * D_MODEL
IN_CONV = N_HEADS * HEAD_DIM + 2 * KV_DIM
IN_GATE = IN_CONV + 3 * D_MODEL
IN_COLS = IN_GATE + 2 * D_MODEL
EPS = 1e-6
ATTN_SCALE = HEAD_DIM ** -0.5
MEM_SCALE = (MEM_DIM // MEM_HEADS) ** -0.5
CONV_TAPS = 3
CONV_LEFT = CONV_TAPS - 1

LANES = 128
SUBLANES = 8
HALF = LANES // 2
SCORE_COLS = 2 * LANES
MASKED = -1e30
V7X_VMEM_BYTES = 64 * 1024 * 1024
VMEM_LIMIT = V7X_VMEM_BYTES - 8 * 1024 * 1024

PROMPT_TILE = 512
FFN_TILE = 512
SAMPLE_SEQS = 8
COLS = 256
FFN_COLS = 512

BF16 = jnp.bfloat16
F32 = jnp.float32


def _dot(a, b):
    return jnp.dot(a, b, preferred_element_type=F32)


def _dot_nt(a, b):
    return lax.dot_general(a, b, (((1,), (1,)), ((), ())), preferred_element_type=F32)


def _rms(x, g):
    return x * lax.rsqrt(jnp.mean(x * x, axis=-1, keepdims=True) + EPS) * g


def _low_half(shape):
    return lax.broadcasted_iota(jnp.int32, shape, len(shape) - 1) < HALF


def _dup_halves(col):
    swapped = pltpu.roll(col, HALF, axis=1)
    lo = _low_half(col.shape)
    return jnp.where(lo, col, swapped), jnp.where(lo, swapped, col)


def _store_lane_tiled(dst_ref, kv, store):
    for c in range(KV_DIM // LANES):
        even, odd = _dup_halves(kv[:, c * LANES:(c + 1) * LANES])
        store(dst_ref, 2 * c, even)
        store(dst_ref, 2 * c + 1, odd)


def _pad_keys(block):
    pad = jnp.zeros((SCORE_COLS - block.shape[0], block.shape[1]), block.dtype)
    return jnp.concatenate([block, pad], axis=0)


def _head_rows(qa, qb):
    lo = _low_half(qa.shape)
    zero = jnp.zeros_like(qa)
    blocks = (
        jnp.concatenate([jnp.where(lo, qa, zero), zero], axis=1),
        jnp.concatenate([jnp.where(lo, zero, qa), zero], axis=1),
        jnp.concatenate([zero, jnp.where(lo, qb, zero)], axis=1),
        jnp.concatenate([zero, jnp.where(lo, zero, qb)], axis=1),
    )
    return jnp.concatenate(blocks, axis=0)


def _pick_heads(o4, rows):
    lo = _low_half((rows, LANES))
    oa = jnp.where(lo, o4[0:rows, 0:LANES], o4[rows:2 * rows, 0:LANES])
    ob = jnp.where(lo, o4[2 * rows:3 * rows, LANES:], o4[3 * rows:4 * rows, LANES:])
    return oa, ob


def _build_score_bias(bias_ref, slopes_ref, sink_ref, layer, nq, nk, first_valid_keys):
    qi = lax.broadcasted_iota(jnp.int32, (nq, SCORE_COLS), 0)
    ji = lax.broadcasted_iota(jnp.int32, (nq, SCORE_COLS), 1)
    dist = jnp.abs(qi + WINDOW - ji).astype(F32)
    for h in range(N_HEADS):
        base = jnp.where(ji == nk, sink_ref[layer, h], -slopes_ref[h] * dist)
        for v, first in enumerate(first_valid_keys):
            valid = (ji >= first) & (ji <= nk)
            bias_ref[v, h * nq:(h + 1) * nq, :] = jnp.where(valid, base, MASKED)


def _softmax(s):
    p = jnp.exp(s - jnp.max(s, axis=-1, keepdims=True))
    return p * (1.0 / jnp.sum(p, axis=-1, keepdims=True))


def _group_attention_unit(q_s, a_s, r0, nq, kv_head, kblock_of, vblock_of, bias_of):
    c0 = kv_head * GROUP * HEAD_DIM

    def scores():
        qs = _head_rows(q_s[r0:r0 + nq, c0:c0 + LANES], q_s[r0:r0 + nq, c0 + LANES:c0 + 2 * LANES])
        return _dot_nt(qs, _pad_keys(kblock_of())) + bias_of()

    def values(p):
        oa, ob = _pick_heads(_dot(p.astype(BF16), _pad_keys(vblock_of())), nq)
        a_s[r0:r0 + nq, c0:c0 + LANES] = oa.astype(BF16)
        a_s[r0:r0 + nq, c0 + LANES:c0 + 2 * LANES] = ob.astype(BF16)

    return scores, _softmax, values


def _emit_pipelined(units, side):
    slots = len(units) + 2
    total = sum(cost for cost, _ in side)
    emitted = 0
    state = {}
    for j in range(slots):
        if j < len(units):
            state[j] = units[j][0]()
        if 0 <= j - 1 < len(units):
            state[j - 1] = units[j - 1][1](state[j - 1])
        while side and emitted * slots < (j + 1) * total:
            cost, emit = side.pop(0)
            emit()
            emitted += cost
        if 0 <= j - 2 < len(units):
            units[j - 2][2](state.pop(j - 2))
    for _, emit in side:
        emit()


def _q_chunk(c, h_s, q_s, wqkv_ref, scale):
    q_s[:, c:c + COLS] = (_dot(h_s[...], wqkv_ref[:, c:c + COLS]) * scale).astype(BF16)


def _project_kv(h_s, wqkv_ref):
    hb = h_s[...]
    k = _dot(hb, wqkv_ref[:, D_MODEL:D_MODEL + KV_DIM])
    v = _dot(hb, wqkv_ref[:, D_MODEL + KV_DIM:D_MODEL + 2 * KV_DIM])
    return k, v


def _gate_tasks(c, h_s, gate_s, wgate_ref):
    def gate(which):
        w0 = which * D_MODEL + c
        gate_s[which, :, c:c + COLS] = jax.nn.sigmoid(_dot(h_s[...], wgate_ref[:, w0:w0 + COLS]))

    return [(COLS, functools.partial(gate, 0)), (COLS, functools.partial(gate, 1))]


def _conv_tasks(c, h_s, boin_s, u3, rows, co_ref, wconv_ref, convw_ref, next_left_ref=None):
    groups = u3.shape[0]
    total = groups * rows
    cs = slice(c, c + COLS)

    def conv_input():
        hb = h_s[...]
        ch = _dot(hb, wconv_ref[:, c:c + COLS])
        cc = _dot(hb, wconv_ref[:, 2 * D_MODEL + c:2 * D_MODEL + c + COLS])
        u3[:, SUBLANES:SUBLANES + rows, cs] = (cc * ch).reshape(groups, rows, COLS)
        tail = u3[:, SUBLANES + rows - CONV_LEFT:SUBLANES + rows, cs]
        co_ref[:, :, cs] = tail
        if next_left_ref is not None:
            next_left_ref[:, SUBLANES - CONV_LEFT:SUBLANES, cs] = tail

    def conv_output():
        taps = [u3[:, SUBLANES - CONV_LEFT + j:SUBLANES - CONV_LEFT + j + rows, cs].reshape(total, COLS)
                for j in range(CONV_TAPS)]
        conv = taps[0] * convw_ref[0:1, cs] + taps[1] * convw_ref[1:2, cs] + taps[2] * convw_ref[2:3, cs]
        cb = _dot(h_s[...], wconv_ref[:, D_MODEL + c:D_MODEL + c + COLS])
        boin_s[:, cs] = (cb * conv).astype(BF16)

    return [(2 * COLS, conv_input), (COLS, conv_output)]


def _side_tasks(h_s, boin_s, gate_s, u3, rows, co_ref, wconv_ref, wgate_ref, convw_ref, next_left_ref=None):
    tasks = []
    for c in range(0, D_MODEL, COLS):
        tasks += _conv_tasks(c, h_s, boin_s, u3, rows, co_ref, wconv_ref, convw_ref, next_left_ref)
    for c in range(0, D_MODEL, COLS):
        tasks += _gate_tasks(c, h_s, gate_s, wgate_ref)
    return tasks


def _merge(a_s, boin_s, gate_s, mix_s, x_ref, xo_ref, wao_ref, wco_ref, wmo_ref, anchor=None):
    for c in range(0, D_MODEL, COLS):
        cs = slice(c, c + COLS)
        a = _dot(a_s[...], wao_ref[:, cs])
        bo = _dot(boin_s[...], wco_ref[:, cs])
        mixed = (gate_s[0, :, cs] * a + gate_s[1, :, cs] * bo).astype(BF16)
        if anchor is not None:
            never, value = anchor
            mixed = jnp.where(never, value[:, cs], mixed)
        mix_s[:, cs] = mixed
    for c in range(0, D_MODEL, COLS):
        cs = slice(c, c + COLS)
        xo_ref[:, cs] = x_ref[:, cs] + _dot(mix_s[...], wmo_ref[:, cs])


BAND_HEAD_ORDER = (0, 2, 1, 3)
BAND_KEYS = WINDOW + CHUNK
START_VARIANTS = WIN_CHUNKS + 1
BLOCKS_PER_STAGE = 2
LOG2E = 1.4426950408889634
BAND_Q_SCALE = ATTN_SCALE * LOG2E


def _build_band_bias(bias_ref, slopes_ref):
    ji = lax.broadcasted_iota(jnp.int32, (BAND_KEYS, 2 * LANES), 0)
    li = lax.broadcasted_iota(jnp.int32, (BAND_KEYS, 2 * LANES), 1)
    dist = jnp.abs(li % CHUNK + WINDOW - ji).astype(F32)
    slot = li // CHUNK
    for kh in range(N_KV_HEADS):
        slope = jnp.zeros((BAND_KEYS, 2 * LANES), F32)
        for s, g in enumerate(BAND_HEAD_ORDER):
            slope = jnp.where(slot == s, slopes_ref[kh * GROUP + g], slope)
        base = -slope * dist * LOG2E
        rows = slice(kh * BAND_KEYS, (kh + 1) * BAND_KEYS)
        for v in range(START_VARIANTS):
            bias_ref[v, rows, :] = jnp.where(ji >= (WIN_CHUNKS - v) * CHUNK, base, MASKED)


def _band_query_rows(qa, qb):
    lo = _low_half(qa.shape)
    zero = jnp.zeros_like(qa)
    blocks = (
        jnp.concatenate([jnp.where(lo, qa, zero), zero], axis=1),
        jnp.concatenate([zero, jnp.where(lo, qb, zero)], axis=1),
        jnp.concatenate([jnp.where(lo, zero, qa), zero], axis=1),
        jnp.concatenate([zero, jnp.where(lo, zero, qb)], axis=1),
    )
    return jnp.concatenate(blocks, axis=0)


def _band_scores(q_s, r0, kv_head, kblock, bias):
    c0 = kv_head * GROUP * HEAD_DIM
    qs = _band_query_rows(q_s[r0:r0 + CHUNK, c0:c0 + LANES], q_s[r0:r0 + CHUNK, c0 + LANES:c0 + 2 * LANES])
    return _dot_nt(kblock, qs) + bias


def _band_softmax(st, kv_head, sink_of):
    lo = _low_half((1, LANES))
    probs, inv_den = [], []
    for col in range(2):
        s = st[:, col * LANES:(col + 1) * LANES]
        ha, hb = BAND_HEAD_ORDER[2 * col], BAND_HEAD_ORDER[2 * col + 1]
        sink = jnp.where(lo, sink_of(kv_head * GROUP + ha), sink_of(kv_head * GROUP + hb)) * LOG2E
        m = jnp.maximum(jnp.max(s, axis=0, keepdims=True), sink)
        p = jnp.exp2(s - m)
        inv_den.append(1.0 / (jnp.sum(p, axis=0, keepdims=True) + jnp.exp2(sink - m)))
        probs.append(p.astype(BF16))
    return jnp.concatenate(probs, axis=1), inv_den


def _band_values(vt_block, lead, pt):
    parts = [pt]
    if lead:
        parts.insert(0, jnp.zeros((lead, 2 * LANES), BF16))
    if vt_block.shape[1] > lead + BAND_KEYS:
        parts.append(jnp.zeros((vt_block.shape[1] - lead - BAND_KEYS, 2 * LANES), BF16))
    return _dot(vt_block, jnp.concatenate(parts, axis=0))


def _band_store(a_s, r0, kv_head, ot, inv_den):
    c0 = kv_head * GROUP * HEAD_DIM
    stacked = jnp.concatenate([ot[:, 0:LANES] * inv_den[0], ot[:, LANES:] * inv_den[1]], axis=0)
    out = stacked.T.astype(BF16)
    a_s[r0:r0 + CHUNK, c0:c0 + LANES] = out[0:CHUNK]
    a_s[r0:r0 + CHUNK, c0 + LANES:c0 + 2 * LANES] = out[CHUNK:]


def _mixer_prompt_kernel(slopes_ref, sink_ref, x_ref, xn_ref, g_ref, win_ref, convw_ref,
                         wao_ref, wco_ref, wmo_ref,
                         xo_ref, ko_ref, vo_ref, co_ref,
                         h2_s, q_s, k4_s, vt_s, a_s, u_s, boin_s, gate_s, mix_s, bias_s,
                         kprev_s, vtprev_s, uprev_s, *, layer, tile):
    wqkv_ref, wconv_ref, wgate_ref = _split_w_in(win_ref)
    t = pl.program_id(1)
    cur = t % 2
    nxt = 1 - cur

    @pl.when(t == 0)
    def _():
        _build_band_bias(bias_s, slopes_ref)
        kprev_s[cur] = jnp.zeros(kprev_s.shape[1:], BF16)
        vtprev_s[cur] = jnp.zeros(vtprev_s.shape[1:], BF16)
        uprev_s[cur] = jnp.zeros(uprev_s.shape[1:], F32)
        h2_s[cur] = _rms(x_ref[...], g_ref[...]).astype(BF16)

    h_s = h2_s.at[cur]


    k4_s[:, 0:WINDOW, :] = kprev_s[cur]
    vt_s[:, :, 0:WINDOW] = vtprev_s[cur]
    left = slice(SUBLANES - CONV_LEFT, SUBLANES)
    u_s[:, left, :] = uprev_s[cur, :, left, :]

    side = _side_tasks(h_s, boin_s, gate_s, u_s, tile, co_ref, wconv_ref, wgate_ref, convw_ref, uprev_s.at[nxt])

    k, v = _project_kv(h_s, wqkv_ref)
    for c in range(0, D_MODEL, COLS):
        _q_chunk(c, h_s, q_s, wqkv_ref, BAND_Q_SCALE)
    for _ in range(2):
        side.pop(0)[1]()
    ko_ref[...] = k[tile - WINDOW:, :]
    vo_ref[...] = v[tile - WINDOW:, :]

    def store_new(dst_ref, head, dup):
        dup = dup.astype(BF16)
        dst_ref[head, WINDOW:WINDOW + tile, 0:LANES] = dup
        dst_ref[head, WINDOW:WINDOW + tile, LANES:] = dup

    _store_lane_tiled(k4_s, k, store_new)
    vt = v.T.astype(BF16)
    for kh in range(N_KV_HEADS):
        vt_s[kh, :, WINDOW:WINDOW + tile] = vt[kh * HEAD_DIM:(kh + 1) * HEAD_DIM, :]

    blocks = [(kh, r0) for kh in range(N_KV_HEADS) for r0 in range(0, tile, CHUNK)]
    stages = [blocks[i:i + BLOCKS_PER_STAGE] for i in range(0, len(blocks), BLOCKS_PER_STAGE)]
    side_per_stage = sum(cost for cost, _ in side) / len(stages)

    sink_of = lambda h: sink_ref[layer, h]

    def scores(kh, r0):
        chunk = r0 // CHUNK
        variant = jnp.where(t == 0, chunk, WIN_CHUNKS) if chunk < WIN_CHUNKS else WIN_CHUNKS
        bias = bias_s[variant, kh * BAND_KEYS:(kh + 1) * BAND_KEYS, :]
        return _band_scores(q_s, r0, kh, k4_s[kh, r0:r0 + BAND_KEYS, :], bias)

    def values(kh, r0, pt):
        lead = r0 % LANES
        return _band_values(vt_s[kh, :, r0 - lead:r0 - lead + 2 * LANES], lead, pt)

    emitted = 0
    st_next = [scores(*b) for b in stages[0]]
    for i, stage in enumerate(stages):
        st = st_next
        if i + 1 < len(stages):
            st_next = [scores(*b) for b in stages[i + 1]]
        soft = [_band_softmax(s, kh, sink_of) for s, (kh, _) in zip(st, stage)]
        while side and emitted < (i + 1) * side_per_stage:
            cost, emit = side.pop(0)
            emit()
            emitted += cost
        outs = [values(kh, r0, pt) for (pt, _), (kh, r0) in zip(soft, stage)]
        for ot, (_, inv_den), (kh, r0) in zip(outs, soft, stage):
            _band_store(a_s, r0, kh, ot, inv_den)
    for _, emit in side:
        emit()

    kprev_s[nxt] = k4_s[:, tile:tile + WINDOW, :]
    vtprev_s[nxt] = vt_s[:, :, tile:tile + WINDOW]

    h_next = _rms(xn_ref[...], g_ref[...]).astype(BF16)
    h2_s[nxt] = h_next
    _merge(a_s, boin_s, gate_s, mix_s, x_ref, xo_ref, wao_ref, wco_ref, wmo_ref, anchor=(t < 0, h_next))


def _mixer_sample_kernel(slopes_ref, sink_ref, x_ref, g_ref, kc_ref, vc_ref, st_ref,
                         win_ref, convw_ref, wao_ref, wco_ref, wmo_ref,
                         xo_ref, ko_ref, vo_ref, co_ref,
                         h_s, q_s, k4_s, v4_s, a_s, u_s, boin_s, gate_s, mix_s, bias_s, *, layer, seqs, rows):
    wqkv_ref, wconv_ref, wgate_ref = _split_w_in(win_ref)
    keys = WINDOW + rows

    @pl.when(pl.program_id(0) == 0)
    def _():
        _build_score_bias(bias_s, slopes_ref, sink_ref, layer, rows, keys, (0,))

    u_s[:, SUBLANES - CONV_LEFT:SUBLANES, :] = st_ref[...]

    h_s[...] = _rms(x_ref[...], g_ref[...]).astype(BF16)
    for c in range(0, D_MODEL, COLS):
        _q_chunk(c, h_s, q_s, wqkv_ref, ATTN_SCALE)
    k, v = _project_kv(h_s, wqkv_ref)
    ko_ref[...] = k
    vo_ref[...] = v

    def store_cache(dst_ref, head, dup):
        dup = dup.reshape(seqs, WINDOW, LANES).astype(BF16)
        dst_ref[head, :, 0:WINDOW, 0:LANES] = dup
        dst_ref[head, :, 0:WINDOW, LANES:] = dup

    def store_new(dst_ref, head, dup):
        dup = dup.reshape(seqs, rows, LANES).astype(BF16)
        dst_ref[head, :, WINDOW:keys, 0:LANES] = dup
        dst_ref[head, :, WINDOW:keys, LANES:] = dup

    _store_lane_tiled(k4_s, kc_ref[...].reshape(seqs * WINDOW, KV_DIM), store_cache)
    _store_lane_tiled(v4_s, vc_ref[...].reshape(seqs * WINDOW, KV_DIM), store_cache)
    _store_lane_tiled(k4_s, k, store_new)
    _store_lane_tiled(v4_s, v, store_new)

    units = [
        _group_attention_unit(
            q_s, a_s, g * rows, rows, kh,
            functools.partial(lambda kh, g: k4_s[kh, g], kh, g),
            functools.partial(lambda kh, g: v4_s[kh, g], kh, g),
            functools.partial(lambda kh: bias_s[0, kh * GROUP * rows:(kh + 1) * GROUP * rows, :], kh))
        for kh in range(N_KV_HEADS) for g in range(seqs)]
    _emit_pipelined(units, _side_tasks(h_s, boin_s, gate_s, u_s, rows, co_ref, wconv_ref, wgate_ref, convw_ref))
    _merge(a_s, boin_s, gate_s, mix_s, x_ref, xo_ref, wao_ref, wco_ref, wmo_ref)


def _cross_attention_units(cq_s, co_s, mk_ref, mv_ref, groups, rows):
    lo = _low_half((rows, LANES))
    zero = jnp.zeros((rows, LANES), BF16)
    stacked = MEM_HEADS * rows <= 2 * LANES
    units = []
    for g in range(groups):
        rs = slice(g * rows, (g + 1) * rows)
        if stacked:
            def scores(rs=rs, g=g):
                return _dot_nt(_head_rows(cq_s[rs, 0:LANES], cq_s[rs, LANES:]), mk_ref[g].astype(BF16))

            def values(p, rs=rs, g=g):
                oa, ob = _pick_heads(_dot(p.astype(BF16), mv_ref[g].astype(BF16)), rows)
                co_s[rs, 0:LANES] = oa.astype(BF16)
                co_s[rs, LANES:] = ob.astype(BF16)

            units.append((scores, _softmax, values))
            continue
        held = {}
        for hh in range(MEM_HEADS):
            c, odd = divmod(hh, 2)

            def scores(rs=rs, g=g, c=c, odd=odd):
                qcol = cq_s[rs, c * LANES:(c + 1) * LANES]
                qm = jnp.where(lo, zero, qcol) if odd else jnp.where(lo, qcol, zero)
                qm = jnp.concatenate([qm, zero] if c == 0 else [zero, qm], axis=1)
                return _dot_nt(qm, mk_ref[g].astype(BF16))

            def values(p, rs=rs, g=g, c=c, odd=odd, held=held):
                out = _dot(p.astype(BF16), mv_ref[g].astype(BF16))[:, c * LANES:(c + 1) * LANES]
                if odd:
                    co_s[rs, c * LANES:(c + 1) * LANES] = jnp.where(lo, held.pop(c), out).astype(BF16)
                else:
                    held[c] = out

            units.append((scores, _softmax, values))
    return units


def _ffn_kernel(x_ref, mk_ref, mv_ref, gc_ref, wcq_ref, wcout_ref, gm_ref, wup_ref, wdown_ref, gf_ref,
                xo_ref,
                h_s, cq_s, co_s, x1_s, hm_s, act_s, *, groups, rows, final):
    i = pl.program_id(0)
    cur = i % 2
    nxt = 1 - cur

    @pl.when(i == 0)
    def _():
        x1_s[cur] = jnp.zeros(x1_s.shape[1:], F32)
        hm_s[cur] = jnp.zeros(hm_s.shape[1:], BF16)

    def norm_in():
        h_s[...] = _rms(x_ref[...], gc_ref[...]).astype(BF16)

    def project_q():
        cq_s[...] = (_dot(h_s[...], wcq_ref[...]) * MEM_SCALE).astype(BF16)

    def residual(c):
        cs = slice(c, c + FFN_COLS)
        x1_s[nxt, :, cs] = x_ref[:, cs] + _dot(co_s[...], wcout_ref[:, cs])

    def norm_mlp():
        hm_s[nxt] = _rms(x1_s[nxt], gm_ref[...]).astype(BF16)

    units = _cross_attention_units(cq_s, co_s, mk_ref, mv_ref, groups, rows)
    pieces = [norm_in, project_q]
    state = {}
    for j in range(len(units) + 2):
        def piece(j=j):
            if j < len(units):
                state[j] = units[j][0]()
            if 0 <= j - 1 < len(units):
                state[j - 1] = units[j - 1][1](state[j - 1])
            if 0 <= j - 2 < len(units):
                units[j - 2][2](state.pop(j - 2))
        pieces.append(piece)
    pieces += [functools.partial(residual, c) for c in range(0, D_MODEL, FFN_COLS)]
    pieces.append(norm_mlp)

    for c in range(0, D_FF, FFN_COLS):
        up = jnp.maximum(_dot(hm_s[cur], wup_ref[:, c:c + FFN_COLS]), 0.0)
        act_s[:, c:c + FFN_COLS] = (up * up).astype(BF16)
        if pieces:
            pieces.pop(0)()
    for piece in pieces:
        piece()
    for c in range(0, D_MODEL, FFN_COLS):
        cs = slice(c, c + FFN_COLS)
        x1_s[cur, :, cs] = x1_s[cur, :, cs] + _dot(act_s[...], wdown_ref[:, cs])
    if final:
        xo_ref[...] = _rms(x1_s[cur], gf_ref[...])
    else:
        xo_ref[...] = x1_s[cur]


def _memkv_kernel(mem_ref, g_ref, w_ref, mk_ref, mv_ref):
    for b in range(mem_ref.shape[0]):
        h = _rms(mem_ref[b], g_ref[...]).astype(BF16)
        kv = _dot(h, w_ref[...])
        mk_ref[b] = kv[:, 0:MEM_DIM]
        mv_ref[b] = kv[:, MEM_DIM:]


def _params(n_axes):
    return pltpu.CompilerParams(dimension_semantics=("arbitrary",) * n_axes, vmem_limit_bytes=VMEM_LIMIT)


def _resident(shape, layer):
    zeros = (0,) * len(shape)
    return pl.BlockSpec((None,) + tuple(shape), lambda *_: (layer,) + zeros, pipeline_mode=pl.Buffered(1))


def _smem():
    return pl.BlockSpec(memory_space=pltpu.SMEM)


def _split_w_in(win_ref):
    return (win_ref.at[:, 0:IN_CONV], win_ref.at[:, IN_CONV:IN_GATE], win_ref.at[:, IN_GATE:IN_COLS])


def _mixer_weight_specs(layer):
    return [
        _resident((D_MODEL, IN_COLS), layer),
        _resident((CONV_TAPS, D_MODEL), layer),
        _resident((D_MODEL, D_MODEL), layer),
        _resident((D_MODEL, D_MODEL), layer),
        _resident((D_MODEL, D_MODEL), layer),
    ]


def _mixer_weights(w):
    return (w["w_in"], w["conv_w"], w["w_attn_out"], w["w_conv_out"], w["w_mix_out"])


def _mixer_prompt(layer, x, slopes, w):
    batch, seq, _ = x.shape
    tile = PROMPT_TILE
    row = lambda b, t: (b, t, 0)
    next_row = lambda b, t: (b, jnp.minimum(t + 1, seq // tile - 1), 0)
    per_seq = lambda b, t: (b, 0, 0)
    return pl.pallas_call(
        functools.partial(_mixer_prompt_kernel, layer=layer, tile=tile),
        grid=(batch, seq // tile),
        in_specs=[_smem(), _smem(),
                  pl.BlockSpec((None, tile, D_MODEL), row),
                  pl.BlockSpec((None, tile, D_MODEL), next_row),
                  _resident((1, D_MODEL), layer)] + _mixer_weight_specs(layer),
        out_specs=[pl.BlockSpec((None, tile, D_MODEL), row),
                   pl.BlockSpec((None, WINDOW, KV_DIM), per_seq),
                   pl.BlockSpec((None, WINDOW, KV_DIM), per_seq),
                   pl.BlockSpec((1, CONV_LEFT, D_MODEL), per_seq)],
        out_shape=[jax.ShapeDtypeStruct(x.shape, F32),
                   jax.ShapeDtypeStruct((batch, WINDOW, KV_DIM), F32),
                   jax.ShapeDtypeStruct((batch, WINDOW, KV_DIM), F32),
                   jax.ShapeDtypeStruct((batch, CONV_LEFT, D_MODEL), F32)],
        scratch_shapes=[
            pltpu.VMEM((2, tile, D_MODEL), BF16),
            pltpu.VMEM((tile, D_MODEL), BF16),
            pltpu.VMEM((N_KV_HEADS, WINDOW + tile, 2 * LANES), BF16),
            pltpu.VMEM((N_KV_HEADS, HEAD_DIM, WINDOW + tile), BF16),
            pltpu.VMEM((tile, D_MODEL), BF16),
            pltpu.VMEM((1, SUBLANES + tile, D_MODEL), F32),
            pltpu.VMEM((tile, D_MODEL), BF16),
            pltpu.VMEM((2, tile, D_MODEL), F32),
            pltpu.VMEM((tile, D_MODEL), BF16),
            pltpu.VMEM((START_VARIANTS, N_KV_HEADS * BAND_KEYS, 2 * LANES), F32),
            pltpu.VMEM((2, N_KV_HEADS, WINDOW, 2 * LANES), BF16),
            pltpu.VMEM((2, N_KV_HEADS, HEAD_DIM, WINDOW), BF16),
            pltpu.VMEM((2, 1, SUBLANES, D_MODEL), F32),
        ],
        compiler_params=_params(2),
        name=f"mixer_prompt_l{layer}",
    )(slopes, w["sink"], x, x, w["g_mix"], *_mixer_weights(w))


def _mixer_sample(layer, x2, slopes, kc, vc, st, w, rows):
    total = x2.shape[0]
    seqs = SAMPLE_SEQS
    tile = seqs * rows
    keys = WINDOW + rows
    row = lambda i: (i, 0)
    per_seq = lambda i: (layer, i, 0, 0)
    return pl.pallas_call(
        functools.partial(_mixer_sample_kernel, layer=layer, seqs=seqs, rows=rows),
        grid=(total // tile,),
        in_specs=[_smem(), _smem(),
                  pl.BlockSpec((tile, D_MODEL), row),
                  _resident((1, D_MODEL), layer),
                  pl.BlockSpec((None, seqs, WINDOW, KV_DIM), per_seq),
                  pl.BlockSpec((None, seqs, WINDOW, KV_DIM), per_seq),
                  pl.BlockSpec((None, seqs, CONV_LEFT, D_MODEL), per_seq),
                  ] + _mixer_weight_specs(layer),
        out_specs=[pl.BlockSpec((tile, D_MODEL), row),
                   pl.BlockSpec((tile, KV_DIM), row),
                   pl.BlockSpec((tile, KV_DIM), row),
                   pl.BlockSpec((seqs, CONV_LEFT, D_MODEL), lambda i: (i, 0, 0))],
        out_shape=[jax.ShapeDtypeStruct(x2.shape, F32),
                   jax.ShapeDtypeStruct((total, KV_DIM), F32),
                   jax.ShapeDtypeStruct((total, KV_DIM), F32),
                   jax.ShapeDtypeStruct((total // rows, CONV_LEFT, D_MODEL), F32)],
        scratch_shapes=[
            pltpu.VMEM((tile, D_MODEL), BF16),
            pltpu.VMEM((tile, D_MODEL), BF16),
            pltpu.VMEM((N_KV_HEADS, seqs, keys, 2 * LANES), BF16),
            pltpu.VMEM((N_KV_HEADS, seqs, keys, 2 * LANES), BF16),
            pltpu.VMEM((tile, D_MODEL), BF16),
            pltpu.VMEM((seqs, SUBLANES + rows, D_MODEL), F32),
            pltpu.VMEM((tile, D_MODEL), BF16),
            pltpu.VMEM((2, tile, D_MODEL), F32),
            pltpu.VMEM((tile, D_MODEL), BF16),
            pltpu.VMEM((1, N_HEADS * rows, SCORE_COLS), F32),
        ],
        compiler_params=_params(1),
        name=f"mixer_sample_l{layer}",
    )(slopes, w["sink"], x2, w["g_mix"], kc, vc, st, *_mixer_weights(w))


def _ffn(layer, x2, mk, mv, w, groups, rows, final, name):
    total = x2.shape[0]
    tile = groups * rows
    steps_per_group = (total // mk.shape[1]) // rows if groups == 1 else 1
    n_tiles = total // tile
    tile_in = lambda i: jnp.minimum(i, n_tiles - 1)
    mem_spec = pl.BlockSpec((None, groups, N_MEM, MEM_DIM),
                            lambda i: (layer, tile_in(i) // steps_per_group, 0, 0))
    return pl.pallas_call(
        functools.partial(_ffn_kernel, groups=groups, rows=rows, final=final),
        grid=(n_tiles + 1,),
        in_specs=[pl.BlockSpec((tile, D_MODEL), lambda i: (tile_in(i), 0)), mem_spec, mem_spec,
                  _resident((1, D_MODEL), layer),
                  _resident((D_MODEL, MEM_DIM), layer),
                  _resident((MEM_DIM, D_MODEL), layer),
                  _resident((1, D_MODEL), layer),
                  _resident((D_MODEL, D_FF), layer),
                  _resident((D_FF, D_MODEL), layer),
                  pl.BlockSpec((1, D_MODEL), lambda i: (0, 0))],
        out_specs=pl.BlockSpec((tile, D_MODEL), lambda i: (jnp.maximum(i - 1, 0), 0)),
        out_shape=jax.ShapeDtypeStruct(x2.shape, F32),
        scratch_shapes=[
            pltpu.VMEM((tile, D_MODEL), BF16),
            pltpu.VMEM((tile, MEM_DIM), BF16),
            pltpu.VMEM((tile, MEM_DIM), BF16),
            pltpu.VMEM((2, tile, D_MODEL), F32),
            pltpu.VMEM((2, tile, D_MODEL), BF16),
            pltpu.VMEM((tile, D_FF), BF16),
        ],
        compiler_params=_params(1),
        name=name,
    )(x2, mk, mv, w["g_cross"], w["w_cq"], w["w_co"], w["g_mlp"], w["w_up"], w["w_down"], w["g_final"])


def _memory_kv(mem, g_mem, w_ckv):
    batch = mem.shape[0]
    out = jax.ShapeDtypeStruct((DEPTH, batch, N_MEM, MEM_DIM), F32)
    return pl.pallas_call(
        _memkv_kernel,
        grid=(DEPTH,),
        in_specs=[pl.BlockSpec((batch, N_MEM, D_MODEL), lambda l: (0, 0, 0)),
                  pl.BlockSpec((None, 1, D_MODEL), lambda l: (l, 0, 0)),
                  pl.BlockSpec((None, D_MODEL, 2 * MEM_DIM), lambda l: (l, 0, 0))],
        out_specs=[pl.BlockSpec((None, batch, N_MEM, MEM_DIM), lambda l: (l, 0, 0, 0))] * 2,
        out_shape=[out, out],
        compiler_params=_params(1),
        name="memory_kv",
    )(mem, g_mem, w_ckv)


def kernel(x_prompt, x_sample, mem_prompt, cache_attn_k, cache_attn_v, state_conv, cache_mem_k, cache_mem_v,
           norm_mix_g, w_in, conv_w, attn_sink, w_attn_out, w_conv_out, w_mix_out, norm_cross_g, norm_mem_g,
           w_cq, w_ckv, w_co, norm_mlp_g, w_up, w_down, norm_final_g):
    batch, seq, _ = x_prompt.shape
    dec_batch, dec_seq, _ = x_sample.shape
    assert seq % PROMPT_TILE == 0 and PROMPT_TILE % LANES == 0 and PROMPT_TILE >= WINDOW
    assert seq % FFN_TILE == 0
    assert dec_seq <= CHUNK and dec_seq % 16 == 0 and dec_batch % SAMPLE_SEQS == 0
    assert cache_attn_k.shape[2] == WINDOW

    w = {
        "sink": attn_sink,
        "g_mix": norm_mix_g[:, None, :],
        "w_in": w_in.astype(BF16),
        "conv_w": conv_w,
        "w_attn_out": w_attn_out.astype(BF16),
        "w_conv_out": w_conv_out.astype(BF16),
        "w_mix_out": w_mix_out.astype(BF16),
        "g_cross": norm_cross_g[:, None, :],
        "w_cq": w_cq.astype(BF16),
        "w_co": w_co.astype(BF16),
        "g_mlp": norm_mlp_g[:, None, :],
        "w_up": w_up.astype(BF16),
        "w_down": w_down.astype(BF16),
        "g_final": norm_final_g[None, :],
    }
    slopes = jnp.exp2(-8.0 * jnp.arange(1, N_HEADS + 1, dtype=F32) / N_HEADS)

    mk_p, mv_p = _memory_kv(mem_prompt, norm_mem_g[:, None, :], w_ckv.astype(BF16))

    kc = cache_attn_k.reshape(DEPTH, dec_batch, WINDOW, KV_DIM)
    vc = cache_attn_v.reshape(DEPTH, dec_batch, WINDOW, KV_DIM)
    mk_s = cache_mem_k.reshape(DEPTH, dec_batch, N_MEM, MEM_DIM)
    mv_s = cache_mem_v.reshape(DEPTH, dec_batch, N_MEM, MEM_DIM)

    xp = x_prompt
    xs = x_sample.reshape(dec_batch * dec_seq, D_MODEL)
    kp_l, vp_l, cp_l, ks_l, vs_l, cs_l = [], [], [], [], [], []
    for layer in range(DEPTH):
        final = layer == DEPTH - 1
        xp, kp, vp, cp = _mixer_prompt(layer, xp, slopes, w)
        xp = _ffn(layer, xp.reshape(batch * seq, D_MODEL), mk_p, mv_p, w, 1, FFN_TILE, final,
                  f"ffn_prompt_l{layer}").reshape(batch, seq, D_MODEL)
        xs, kn, vn, cs = _mixer_sample(layer, xs, slopes, kc, vc, state_conv, w, dec_seq)
        xs = _ffn(layer, xs, mk_s, mv_s, w, SAMPLE_SEQS, dec_seq, final, f"ffn_sample_l{layer}")
        kp_l.append(kp)
        vp_l.append(vp)
        cp_l.append(cp)
        ks_l.append(kn)
        vs_l.append(vn)
        cs_l.append(cs)

    kv_p = (DEPTH, batch, WINDOW, N_KV_HEADS, HEAD_DIM)
    kv_s = (DEPTH, dec_batch, dec_seq, N_KV_HEADS, HEAD_DIM)
    mem_shape = (DEPTH, batch, N_MEM, MEM_HEADS, MEM_DIM // MEM_HEADS)
    return (xp, xs.reshape(dec_batch, dec_seq, D_MODEL),
            jnp.stack(kp_l).reshape(kv_p), jnp.stack(vp_l).reshape(kv_p), jnp.stack(cp_l),
            mk_p.reshape(mem_shape), mv_p.reshape(mem_shape),
            jnp.stack(ks_l).reshape(kv_s), jnp.stack(vs_l).reshape(kv_s), jnp.stack(cs_l))
```

```python
import functools

import jax
import jax.numpy as jnp
from jax import lax
from jax.experimental import pallas as pl
from jax.experimental.pallas import tpu as pltpu

D_MODEL = 1024
DEPTH = 4
CHUNK = 64
WINDOW = 128
WIN_CHUNKS = WINDOW // CHUNK
HEAD_DIM = 64
N_HEADS = 16
N_KV_HEADS = 4
GROUP = N_HEADS // N_KV_HEADS
KV_DIM = N_KV_HEADS * HEAD_DIM
N_MEM = 256
MEM_HEADS = 4
MEM_DIM = 256
D_FF = 4 * D_MODEL
IN_CONV = N_HEADS * HEAD_DIM + 2 * KV_DIM
IN_GATE = IN_CONV + 3 * D_MODEL
IN_COLS = IN_GATE + 2 * D_MODEL
EPS = 1e-6
ATTN_SCALE = HEAD_DIM ** -0.5
MEM_SCALE = (MEM_DIM // MEM_HEADS) ** -0.5
CONV_TAPS = 3
CONV_LEFT = CONV_TAPS - 1

LANES = 128
SUBLANES = 8
HALF = LANES // 2
SCORE_COLS = 2 * LANES
MASKED = -1e30
V7X_VMEM_BYTES = 64 * 1024 * 1024
VMEM_LIMIT = V7X_VMEM_BYTES - 8 * 1024 * 1024

PROMPT_TILE = 512
FFN_TILE = 512
SAMPLE_SEQS = 8
COLS = 256
FFN_COLS = 512

BF16 = jnp.bfloat16
F32 = jnp.float32


def _dot(a, b):
    return jnp.dot(a, b, preferred_element_type=F32)


def _dot_nt(a, b):
    return lax.dot_general(a, b, (((1,), (1,)), ((), ())), preferred_element_type=F32)


def _rms(x, g):
    return x * lax.rsqrt(jnp.mean(x * x, axis=-1, keepdims=True) + EPS) * g


def _low_half(shape):
    return lax.broadcasted_iota(jnp.int32, shape, len(shape) - 1) < HALF


def _dup_halves(col):
    swapped = pltpu.roll(col, HALF, axis=1)
    lo = _low_half(col.shape)
    return jnp.where(lo, col, swapped), jnp.where(lo, swapped, col)


def _store_lane_tiled(dst_ref, kv, store):
    for c in range(KV_DIM // LANES):
        even, odd = _dup_halves(kv[:, c * LANES:(c + 1) * LANES])
        store(dst_ref, 2 * c, even)
        store(dst_ref, 2 * c + 1, odd)


def _pad_keys(block):
    pad = jnp.zeros((SCORE_COLS - block.shape[0], block.shape[1]), block.dtype)
    return jnp.concatenate([block, pad], axis=0)


def _head_rows(qa, qb):
    lo = _low_half(qa.shape)
    zero = jnp.zeros_like(qa)
    blocks = (
        jnp.concatenate([jnp.where(lo, qa, zero), zero], axis=1),
        jnp.concatenate([jnp.where(lo, zero, qa), zero], axis=1),
        jnp.concatenate([zero, jnp.where(lo, qb, zero)], axis=1),
        jnp.concatenate([zero, jnp.where(lo, zero, qb)], axis=1),
    )
    return jnp.concatenate(blocks, axis=0)


def _pick_heads(o4, rows):
    lo = _low_half((rows, LANES))
    oa = jnp.where(lo, o4[0:rows, 0:LANES], o4[rows:2 * rows, 0:LANES])
    ob = jnp.where(lo, o4[2 * rows:3 * rows, LANES:], o4[3 * rows:4 * rows, LANES:])
    return oa, ob


def _build_score_bias(bias_ref, slopes_ref, sink_ref, layer, nq, nk, first_valid_keys):
    qi = lax.broadcasted_iota(jnp.int32, (nq, SCORE_COLS), 0)
    ji = lax.broadcasted_iota(jnp.int32, (nq, SCORE_COLS), 1)
    dist = jnp.abs(qi + WINDOW - ji).astype(F32)
    for h in range(N_HEADS):
        base = jnp.where(ji == nk, sink_ref[layer, h], -slopes_ref[h] * dist)
        for v, first in enumerate(first_valid_keys):
            valid = (ji >= first) & (ji <= nk)
            bias_ref[v, h * nq:(h + 1) * nq, :] = jnp.where(valid, base, MASKED)


def _softmax(s):
    p = jnp.exp(s - jnp.max(s, axis=-1, keepdims=True))
    return p * (1.0 / jnp.sum(p, axis=-1, keepdims=True))


def _group_attention_unit(q_s, a_s, r0, nq, kv_head, kblock_of, vblock_of, bias_of):
    c0 = kv_head * GROUP * HEAD_DIM

    def scores():
        qs = _head_rows(q_s[r0:r0 + nq, c0:c0 + LANES], q_s[r0:r0 + nq, c0 + LANES:c0 + 2 * LANES])
        return _dot_nt(qs, _pad_keys(kblock_of())) + bias_of()

    def values(p):
        oa, ob = _pick_heads(_dot(p.astype(BF16), _pad_keys(vblock_of())), nq)
        a_s[r0:r0 + nq, c0:c0 + LANES] = oa.astype(BF16)
        a_s[r0:r0 + nq, c0 + LANES:c0 + 2 * LANES] = ob.astype(BF16)

    return scores, _softmax, values


def _emit_pipelined(units, side):
    slots = len(units) + 2
    total = sum(cost for cost, _ in side)
    emitted = 0
    state = {}
    for j in range(slots):
        if j < len(units):
            state[j] = units[j][0]()
        if 0 <= j - 1 < len(units):
            state[j - 1] = units[j - 1][1](state[j - 1])
        while side and emitted * slots < (j + 1) * total:
            cost, emit = side.pop(0)
            emit()
            emitted += cost
        if 0 <= j - 2 < len(units):
            units[j - 2][2](state.pop(j - 2))
    for _, emit in side:
        emit()


def _q_chunk(c, h_s, q_s, wqkv_ref, scale):
    q_s[:, c:c + COLS] = (_dot(h_s[...], wqkv_ref[:, c:c + COLS]) * scale).astype(BF16)


def _project_kv(h_s, wqkv_ref):
    hb = h_s[...]
    k = _dot(hb, wqkv_ref[:, D_MODEL:D_MODEL + KV_DIM])
    v = _dot(hb, wqkv_ref[:, D_MODEL + KV_DIM:D_MODEL + 2 * KV_DIM])
    return k, v


def _gate_tasks(c, h_s, gate_s, wgate_ref):
    def gate(which):
        w0 = which * D_MODEL + c
        gate_s[which, :, c:c + COLS] = jax.nn.sigmoid(_dot(h_s[...], wgate_ref[:, w0:w0 + COLS]))

    return [(COLS, functools.partial(gate, 0)), (COLS, functools.partial(gate, 1))]


def _conv_tasks(c, h_s, boin_s, u3, rows, co_ref, wconv_ref, convw_ref, next_left_ref=None):
    groups = u3.shape[0]
    total = groups * rows
    cs = slice(c, c + COLS)

    def conv_input():
        hb = h_s[...]
        ch = _dot(hb, wconv_ref[:, c:c + COLS])
        cc = _dot(hb, wconv_ref[:, 2 * D_MODEL + c:2 * D_MODEL + c + COLS])
        u3[:, SUBLANES:SUBLANES + rows, cs] = (cc * ch).reshape(groups, rows, COLS)
        tail = u3[:, SUBLANES + rows - CONV_LEFT:SUBLANES + rows, cs]
        co_ref[:, :, cs] = tail
        if next_left_ref is not None:
            next_left_ref[:, SUBLANES - CONV_LEFT:SUBLANES, cs] = tail

    def conv_output():
        taps = [u3[:, SUBLANES - CONV_LEFT + j:SUBLANES - CONV_LEFT + j + rows, cs].reshape(total, COLS)
                for j in range(CONV_TAPS)]
        conv = taps[0] * convw_ref[0:1, cs] + taps[1] * convw_ref[1:2, cs] + taps[2] * convw_ref[2:3, cs]
        cb = _dot(h_s[...], wconv_ref[:, D_MODEL + c:D_MODEL + c + COLS])
        boin_s[:, cs] = (cb * conv).astype(BF16)

    return [(2 * COLS, conv_input), (COLS, conv_output)]


def _side_tasks(h_s, boin_s, gate_s, u3, rows, co_ref, wconv_ref, wgate_ref, convw_ref, next_left_ref=None):
    tasks = []
    for c in range(0, D_MODEL, COLS):
        tasks += _conv_tasks(c, h_s, boin_s, u3, rows, co_ref, wconv_ref, convw_ref, next_left_ref)
    for c in range(0, D_MODEL, COLS):
        tasks += _gate_tasks(c, h_s, gate_s, wgate_ref)
    return tasks


def _merge(a_s, boin_s, gate_s, mix_s, x_ref, xo_ref, wao_ref, wco_ref, wmo_ref, anchor=None):
    for c in range(0, D_MODEL, COLS):
        cs = slice(c, c + COLS)
        a = _dot(a_s[...], wao_ref[:, cs])
        bo = _dot(boin_s[...], wco_ref[:, cs])
        mixed = (gate_s[0, :, cs] * a + gate_s[1, :, cs] * bo).astype(BF16)
        if anchor is not None:
            never, value = anchor
            mixed = jnp.where(never, value[:, cs], mixed)
        mix_s[:, cs] = mixed
    for c in range(0, D_MODEL, COLS):
        cs = slice(c, c + COLS)
        xo_ref[:, cs] = x_ref[:, cs] + _dot(mix_s[...], wmo_ref[:, cs])


BAND_HEAD_ORDER = (0, 2, 1, 3)
BAND_KEYS = WINDOW + CHUNK
START_VARIANTS = WIN_CHUNKS + 1
BLOCKS_PER_STAGE = 2
LOG2E = 1.4426950408889634
BAND_Q_SCALE = ATTN_SCALE * LOG2E


def _build_band_bias(bias_ref, slopes_ref):
    ji = lax.broadcasted_iota(jnp.int32, (BAND_KEYS, 2 * LANES), 0)
    li = lax.broadcasted_iota(jnp.int32, (BAND_KEYS, 2 * LANES), 1)
    dist = jnp.abs(li % CHUNK + WINDOW - ji).astype(F32)
    slot = li // CHUNK
    for kh in range(N_KV_HEADS):
        slope = jnp.zeros((BAND_KEYS, 2 * LANES), F32)
        for s, g in enumerate(BAND_HEAD_ORDER):
            slope = jnp.where(slot == s, slopes_ref[kh * GROUP + g], slope)
        base = -slope * dist * LOG2E
        rows = slice(kh * BAND_KEYS, (kh + 1) * BAND_KEYS)
        for v in range(START_VARIANTS):
            bias_ref[v, rows, :] = jnp.where(ji >= (WIN_CHUNKS - v) * CHUNK, base, MASKED)


def _band_query_rows(qa, qb):
    lo = _low_half(qa.shape)
    zero = jnp.zeros_like(qa)
    blocks = (
        jnp.concatenate([jnp.where(lo, qa, zero), zero], axis=1),
        jnp.concatenate([zero, jnp.where(lo, qb, zero)], axis=1),
        jnp.concatenate([jnp.where(lo, zero, qa), zero], axis=1),
        jnp.concatenate([zero, jnp.where(lo, zero, qb)], axis=1),
    )
    return jnp.concatenate(blocks, axis=0)


def _band_scores(q_s, r0, kv_head, kblock, bias):
    c0 = kv_head * GROUP * HEAD_DIM
    qs = _band_query_rows(q_s[r0:r0 + CHUNK, c0:c0 + LANES], q_s[r0:r0 + CHUNK, c0 + LANES:c0 + 2 * LANES])
    return _dot_nt(kblock, qs) + bias


def _band_softmax(st, kv_head, sink_of):
    lo = _low_half((1, LANES))
    probs, inv_den = [], []
    for col in range(2):
        s = st[:, col * LANES:(col + 1) * LANES]
        ha, hb = BAND_HEAD_ORDER[2 * col], BAND_HEAD_ORDER[2 * col + 1]
        sink = jnp.where(lo, sink_of(kv_head * GROUP + ha), sink_of(kv_head * GROUP + hb)) * LOG2E
        m = jnp.maximum(jnp.max(s, axis=0, keepdims=True), sink)
        p = jnp.exp2(s - m)
        inv_den.append(1.0 / (jnp.sum(p, axis=0, keepdims=True) + jnp.exp2(sink - m)))
        probs.append(p.astype(BF16))
    return jnp.concatenate(probs, axis=1), inv_den


def _band_values(vt_block, lead, pt):
    parts = [pt]
    if lead:
        parts.insert(0, jnp.zeros((lead, 2 * LANES), BF16))
    if vt_block.shape[1] > lead + BAND_KEYS:
        parts.append(jnp.zeros((vt_block.shape[1] - lead - BAND_KEYS, 2 * LANES), BF16))
    return _dot(vt_block, jnp.concatenate(parts, axis=0))


def _band_store(a_s, r0, kv_head, ot, inv_den):
    c0 = kv_head * GROUP * HEAD_DIM
    stacked = jnp.concatenate([ot[:, 0:LANES] * inv_den[0], ot[:, LANES:] * inv_den[1]], axis=0)
    out = stacked.T.astype(BF16)
    a_s[r0:r0 + CHUNK, c0:c0 + LANES] = out[0:CHUNK]
    a_s[r0:r0 + CHUNK, c0 + LANES:c0 + 2 * LANES] = out[CHUNK:]


def _mixer_prompt_kernel(slopes_ref, sink_ref, x_ref, xn_ref, g_ref, win_ref, convw_ref,
                         wao_ref, wco_ref, wmo_ref,
                         xo_ref, ko_ref, vo_ref, co_ref,
                         h2_s, q_s, k4_s, vt_s, a_s, u_s, boin_s, gate_s, mix_s, bias_s,
                         kprev_s, vtprev_s, uprev_s, *, layer, tile):
    wqkv_ref, wconv_ref, wgate_ref = _split_w_in(win_ref)
    t = pl.program_id(1)
    cur = t % 2
    nxt = 1 - cur

    @pl.when(t == 0)
    def _():
        _build_band_bias(bias_s, slopes_ref)
        kprev_s[cur] = jnp.zeros(kprev_s.shape[1:], BF16)
        vtprev_s[cur] = jnp.zeros(vtprev_s.shape[1:], BF16)
        uprev_s[cur] = jnp.zeros(uprev_s.shape[1:], F32)
        h2_s[cur] = _rms(x_ref[...], g_ref[...]).astype(BF16)

    h_s = h2_s.at[cur]


    k4_s[:, 0:WINDOW, :] = kprev_s[cur]
    vt_s[:, :, 0:WINDOW] = vtprev_s[cur]
    left = slice(SUBLANES - CONV_LEFT, SUBLANES)
    u_s[:, left, :] = uprev_s[cur, :, left, :]

    side = _side_tasks(h_s, boin_s, gate_s, u_s, tile, co_ref, wconv_ref, wgate_ref, convw_ref, uprev_s.at[nxt])

    k, v = _project_kv(h_s, wqkv_ref)
    for c in range(0, D_MODEL, COLS):
        _q_chunk(c, h_s, q_s, wqkv_ref, BAND_Q_SCALE)
    for _ in range(2):
        side.pop(0)[1]()
    ko_ref[...] = k[tile - WINDOW:, :]
    vo_ref[...] = v[tile - WINDOW:, :]

    def store_new(dst_ref, head, dup):
        dup = dup.astype(BF16)
        dst_ref[head, WINDOW:WINDOW + tile, 0:LANES] = dup
        dst_ref[head, WINDOW:WINDOW + tile, LANES:] = dup

    _store_lane_tiled(k4_s, k, store_new)
    vt = v.T.astype(BF16)
    for kh in range(N_KV_HEADS):
        vt_s[kh, :, WINDOW:WINDOW + tile] = vt[kh * HEAD_DIM:(kh + 1) * HEAD_DIM, :]

    blocks = [(kh, r0) for kh in range(N_KV_HEADS) for r0 in range(0, tile, CHUNK)]
    stages = [blocks[i:i + BLOCKS_PER_STAGE] for i in range(0, len(blocks), BLOCKS_PER_STAGE)]
    side_per_stage = sum(cost for cost, _ in side) / len(stages)

    sink_of = lambda h: sink_ref[layer, h]

    def scores(kh, r0):
        chunk = r0 // CHUNK
        variant = jnp.where(t == 0, chunk, WIN_CHUNKS) if chunk < WIN_CHUNKS else WIN_CHUNKS
        bias = bias_s[variant, kh * BAND_KEYS:(kh + 1) * BAND_KEYS, :]
        return _band_scores(q_s, r0, kh, k4_s[kh, r0:r0 + BAND_KEYS, :], bias)

    def values(kh, r0, pt):
        lead = r0 % LANES
        return _band_values(vt_s[kh, :, r0 - lead:r0 - lead + 2 * LANES], lead, pt)

    emitted = 0
    st_next = [scores(*b) for b in stages[0]]
    for i, stage in enumerate(stages):
        st = st_next
        if i + 1 < len(stages):
            st_next = [scores(*b) for b in stages[i + 1]]
        soft = [_band_softmax(s, kh, sink_of) for s, (kh, _) in zip(st, stage)]
        while side and emitted < (i + 1) * side_per_stage:
            cost, emit = side.pop(0)
            emit()
            emitted += cost
        outs = [values(kh, r0, pt) for (pt, _), (kh, r0) in zip(soft, stage)]
        for ot, (_, inv_den), (kh, r0) in zip(outs, soft, stage):
            _band_store(a_s, r0, kh, ot, inv_den)
    for _, emit in side:
        emit()

    kprev_s[nxt] = k4_s[:, tile:tile + WINDOW, :]
    vtprev_s[nxt] = vt_s[:, :, tile:tile + WINDOW]

    h_next = _rms(xn_ref[...], g_ref[...]).astype(BF16)
    h2_s[nxt] = h_next
    _merge(a_s, boin_s, gate_s, mix_s, x_ref, xo_ref, wao_ref, wco_ref, wmo_ref, anchor=(t < 0, h_next))


def _mixer_sample_kernel(slopes_ref, sink_ref, x_ref, g_ref, kc_ref, vc_ref, st_ref,
                         win_ref, convw_ref, wao_ref, wco_ref, wmo_ref,
                         xo_ref, ko_ref, vo_ref, co_ref,
                         h_s, q_s, k4_s, v4_s, a_s, u_s, boin_s, gate_s, mix_s, bias_s, *, layer, seqs, rows):
    wqkv_ref, wconv_ref, wgate_ref = _split_w_in(win_ref)
    keys = WINDOW + rows

    @pl.when(pl.program_id(0) == 0)
    def _():
        _build_score_bias(bias_s, slopes_ref, sink_ref, layer, rows, keys, (0,))

    u_s[:, SUBLANES - CONV_LEFT:SUBLANES, :] = st_ref[...]

    h_s[...] = _rms(x_ref[...], g_ref[...]).astype(BF16)
    for c in range(0, D_MODEL, COLS):
        _q_chunk(c, h_s, q_s, wqkv_ref, ATTN_SCALE)
    k, v = _project_kv(h_s, wqkv_ref)
    ko_ref[...] = k
    vo_ref[...] = v

    def store_cache(dst_ref, head, dup):
        dup = dup.reshape(seqs, WINDOW, LANES).astype(BF16)
        dst_ref[head, :, 0:WINDOW, 0:LANES] = dup
        dst_ref[head, :, 0:WINDOW, LANES:] = dup

    def store_new(dst_ref, head, dup):
        dup = dup.reshape(seqs, rows, LANES).astype(BF16)
        dst_ref[head, :, WINDOW:keys, 0:LANES] = dup
        dst_ref[head, :, WINDOW:keys, LANES:] = dup

    _store_lane_tiled(k4_s, kc_ref[...].reshape(seqs * WINDOW, KV_DIM), store_cache)
    _store_lane_tiled(v4_s, vc_ref[...].reshape(seqs * WINDOW, KV_DIM), store_cache)
    _store_lane_tiled(k4_s, k, store_new)
    _store_lane_tiled(v4_s, v, store_new)

    units = [
        _group_attention_unit(
            q_s, a_s, g * rows, rows, kh,
            functools.partial(lambda kh, g: k4_s[kh, g], kh, g),
            functools.partial(lambda kh, g: v4_s[kh, g], kh, g),
            functools.partial(lambda kh: bias_s[0, kh * GROUP * rows:(kh + 1) * GROUP * rows, :], kh))
        for kh in range(N_KV_HEADS) for g in range(seqs)]
    _emit_pipelined(units, _side_tasks(h_s, boin_s, gate_s, u_s, rows, co_ref, wconv_ref, wgate_ref, convw_ref))
    _merge(a_s, boin_s, gate_s, mix_s, x_ref, xo_ref, wao_ref, wco_ref, wmo_ref)


def _cross_attention_units(cq_s, co_s, mk_ref, mv_ref, groups, rows):
    lo = _low_half((rows, LANES))
    zero = jnp.zeros((rows, LANES), BF16)
    stacked = MEM_HEADS * rows <= 2 * LANES
    units = []
    for g in range(groups):
        rs = slice(g * rows, (g + 1) * rows)
        if stacked:
            def scores(rs=rs, g=g):
                return _dot_nt(_head_rows(cq_s[rs, 0:LANES], cq_s[rs, LANES:]), mk_ref[g].astype(BF16))

            def values(p, rs=rs, g=g):
                oa, ob = _pick_heads(_dot(p.astype(BF16), mv_ref[g].astype(BF16)), rows)
                co_s[rs, 0:LANES] = oa.astype(BF16)
                co_s[rs, LANES:] = ob.astype(BF16)

            units.append((scores, _softmax, values))
            continue
        held = {}
        for hh in range(MEM_HEADS):
            c, odd = divmod(hh, 2)

            def scores(rs=rs, g=g, c=c, odd=odd):
                qcol = cq_s[rs, c * LANES:(c + 1) * LANES]
                qm = jnp.where(lo, zero, qcol) if odd else jnp.where(lo, qcol, zero)
                qm = jnp.concatenate([qm, zero] if c == 0 else [zero, qm], axis=1)
                return _dot_nt(qm, mk_ref[g].astype(BF16))

            def values(p, rs=rs, g=g, c=c, odd=odd, held=held):
                out = _dot(p.astype(BF16), mv_ref[g].astype(BF16))[:, c * LANES:(c + 1) * LANES]
                if odd:
                    co_s[rs, c * LANES:(c + 1) * LANES] = jnp.where(lo, held.pop(c), out).astype(BF16)
                else:
                    held[c] = out

            units.append((scores, _softmax, values))
    return units


def _ffn_kernel(x_ref, mk_ref, mv_ref, gc_ref, wcq_ref, wcout_ref, gm_ref, wup_ref, wdown_ref, gf_ref,
                xo_ref,
                h_s, cq_s, co_s, x1_s, hm_s, act_s, *, groups, rows, final):
    i = pl.program_id(0)
    cur = i % 2
    nxt = 1 - cur

    def phase1_pieces():
        def norm_in():
            h_s[...] = _rms(x_ref[...], gc_ref[...]).astype(BF16)

        def project_q():
            cq_s[...] = (_dot(h_s[...], wcq_ref[...]) * MEM_SCALE).astype(BF16)

        def residual(c):
            cs = slice(c, c + FFN_COLS)
            x1_s[nxt, :, cs] = x_ref[:, cs] + _dot(co_s[...], wcout_ref[:, cs])

        def norm_mlp():
            hm_s[nxt] = _rms(x1_s[nxt], gm_ref[...]).astype(BF16)

        units = _cross_attention_units(cq_s, co_s, mk_ref, mv_ref, groups, rows)
        pieces = [norm_in, project_q]
        state = {}
        for j in range(len(units) + 2):
            def piece(j=j):
                if j < len(units):
                    state[j] = units[j][0]()
                if 0 <= j - 1 < len(units):
                    state[j - 1] = units[j - 1][1](state[j - 1])
                if 0 <= j - 2 < len(units):
                    units[j - 2][2](state.pop(j - 2))
            pieces.append(piece)
        pieces += [functools.partial(residual, c) for c in range(0, D_MODEL, FFN_COLS)]
        pieces.append(norm_mlp)
        return pieces

    @pl.when(i == 0)
    def _():
        for piece in phase1_pieces():
            piece()

    @pl.when(i > 0)
    def _():
        pieces = phase1_pieces()
        for c in range(0, D_FF, FFN_COLS):
            up = jnp.maximum(_dot(hm_s[cur], wup_ref[:, c:c + FFN_COLS]), 0.0)
            act_s[:, c:c + FFN_COLS] = (up * up).astype(BF16)
            if pieces:
                pieces.pop(0)()
        for piece in pieces:
            piece()
        for c in range(0, D_MODEL, FFN_COLS):
            cs = slice(c, c + FFN_COLS)
            x1_s[cur, :, cs] = x1_s[cur, :, cs] + _dot(act_s[...], wdown_ref[:, cs])
        if final:
            xo_ref[...] = _rms(x1_s[cur], gf_ref[...])
        else:
            xo_ref[...] = x1_s[cur]


def _memkv_kernel(mem_ref, g_ref, w_ref, mk_ref, mv_ref):
    for b in range(mem_ref.shape[0]):
        h = _rms(mem_ref[b], g_ref[...]).astype(BF16)
        kv = _dot(h, w_ref[...])
        mk_ref[b] = kv[:, 0:MEM_DIM]
        mv_ref[b] = kv[:, MEM_DIM:]


def _params(n_axes):
    return pltpu.CompilerParams(dimension_semantics=("arbitrary",) * n_axes, vmem_limit_bytes=VMEM_LIMIT)


def _resident(shape, layer):
    zeros = (0,) * len(shape)
    return pl.BlockSpec((None,) + tuple(shape), lambda *_: (layer,) + zeros, pipeline_mode=pl.Buffered(1))


def _smem():
    return pl.BlockSpec(memory_space=pltpu.SMEM)


def _split_w_in(win_ref):
    return (win_ref.at[:, 0:IN_CONV], win_ref.at[:, IN_CONV:IN_GATE], win_ref.at[:, IN_GATE:IN_COLS])


def _mixer_weight_specs(layer):
    return [
        _resident((D_MODEL, IN_COLS), layer),
        _resident((CONV_TAPS, D_MODEL), layer),
        _resident((D_MODEL, D_MODEL), layer),
        _resident((D_MODEL, D_MODEL), layer),
        _resident((D_MODEL, D_MODEL), layer),
    ]


def _mixer_weights(w):
    return (w["w_in"], w["conv_w"], w["w_attn_out"], w["w_conv_out"], w["w_mix_out"])


def _mixer_prompt(layer, x, slopes, w):
    batch, seq, _ = x.shape
    tile = PROMPT_TILE
    row = lambda b, t: (b, t, 0)
    next_row = lambda b, t: (b, jnp.minimum(t + 1, seq // tile - 1), 0)
    per_seq = lambda b, t: (b, 0, 0)
    return pl.pallas_call(
        functools.partial(_mixer_prompt_kernel, layer=layer, tile=tile),
        grid=(batch, seq // tile),
        in_specs=[_smem(), _smem(),
                  pl.BlockSpec((None, tile, D_MODEL), row),
                  pl.BlockSpec((None, tile, D_MODEL), next_row),
                  _resident((1, D_MODEL), layer)] + _mixer_weight_specs(layer),
        out_specs=[pl.BlockSpec((None, tile, D_MODEL), row),
                   pl.BlockSpec((None, WINDOW, KV_DIM), per_seq),
                   pl.BlockSpec((None, WINDOW, KV_DIM), per_seq),
                   pl.BlockSpec((1, CONV_LEFT, D_MODEL), per_seq)],
        out_shape=[jax.ShapeDtypeStruct(x.shape, F32),
                   jax.ShapeDtypeStruct((batch, WINDOW, KV_DIM), F32),
                   jax.ShapeDtypeStruct((batch, WINDOW, KV_DIM), F32),
                   jax.ShapeDtypeStruct((batch, CONV_LEFT, D_MODEL), F32)],
        scratch_shapes=[
            pltpu.VMEM((2, tile, D_MODEL), BF16),
            pltpu.VMEM((tile, D_MODEL), BF16),
            pltpu.VMEM((N_KV_HEADS, WINDOW + tile, 2 * LANES), BF16),
            pltpu.VMEM((N_KV_HEADS, HEAD_DIM, WINDOW + tile), BF16),
            pltpu.VMEM((tile, D_MODEL), BF16),
            pltpu.VMEM((1, SUBLANES + tile, D_MODEL), F32),
            pltpu.VMEM((tile, D_MODEL), BF16),
            pltpu.VMEM((2, tile, D_MODEL), F32),
            pltpu.VMEM((tile, D_MODEL), BF16),
            pltpu.VMEM((START_VARIANTS, N_KV_HEADS * BAND_KEYS, 2 * LANES), F32),
            pltpu.VMEM((2, N_KV_HEADS, WINDOW, 2 * LANES), BF16),
            pltpu.VMEM((2, N_KV_HEADS, HEAD_DIM, WINDOW), BF16),
            pltpu.VMEM((2, 1, SUBLANES, D_MODEL), F32),
        ],
        compiler_params=_params(2),
        name=f"mixer_prompt_l{layer}",
    )(slopes, w["sink"], x, x, w["g_mix"], *_mixer_weights(w))


def _mixer_sample(layer, x2, slopes, kc, vc, st, w, rows):
    total = x2.shape[0]
    seqs = SAMPLE_SEQS
    tile = seqs * rows
    keys = WINDOW + rows
    row = lambda i: (i, 0)
    per_seq = lambda i: (layer, i, 0, 0)
    return pl.pallas_call(
        functools.partial(_mixer_sample_kernel, layer=layer, seqs=seqs, rows=rows),
        grid=(total // tile,),
        in_specs=[_smem(), _smem(),
                  pl.BlockSpec((tile, D_MODEL), row),
                  _resident((1, D_MODEL), layer),
                  pl.BlockSpec((None, seqs, WINDOW, KV_DIM), per_seq),
                  pl.BlockSpec((None, seqs, WINDOW, KV_DIM), per_seq),
                  pl.BlockSpec((None, seqs, CONV_LEFT, D_MODEL), per_seq),
                  ] + _mixer_weight_specs(layer),
        out_specs=[pl.BlockSpec((tile, D_MODEL), row),
                   pl.BlockSpec((tile, KV_DIM), row),
                   pl.BlockSpec((tile, KV_DIM), row),
                   pl.BlockSpec((seqs, CONV_LEFT, D_MODEL), lambda i: (i, 0, 0))],
        out_shape=[jax.ShapeDtypeStruct(x2.shape, F32),
                   jax.ShapeDtypeStruct((total, KV_DIM), F32),
                   jax.ShapeDtypeStruct((total, KV_DIM), F32),
                   jax.ShapeDtypeStruct((total // rows, CONV_LEFT, D_MODEL), F32)],
        scratch_shapes=[
            pltpu.VMEM((tile, D_MODEL), BF16),
            pltpu.VMEM((tile, D_MODEL), BF16),
            pltpu.VMEM((N_KV_HEADS, seqs, keys, 2 * LANES), BF16),
            pltpu.VMEM((N_KV_HEADS, seqs, keys, 2 * LANES), BF16),
            pltpu.VMEM((tile, D_MODEL), BF16),
            pltpu.VMEM((seqs, SUBLANES + rows, D_MODEL), F32),
            pltpu.VMEM((tile, D_MODEL), BF16),
            pltpu.VMEM((2, tile, D_MODEL), F32),
            pltpu.VMEM((tile, D_MODEL), BF16),
            pltpu.VMEM((1, N_HEADS * rows, SCORE_COLS), F32),
        ],
        compiler_params=_params(1),
        name=f"mixer_sample_l{layer}",
    )(slopes, w["sink"], x2, w["g_mix"], kc, vc, st, *_mixer_weights(w))


def _ffn(layer, x2, mk, mv, w, groups, rows, final, name):
    total = x2.shape[0]
    tile = groups * rows
    steps_per_group = (total // mk.shape[1]) // rows if groups == 1 else 1
    n_tiles = total // tile
    tile_in = lambda i: jnp.minimum(i, n_tiles - 1)
    mem_spec = pl.BlockSpec((None, groups, N_MEM, MEM_DIM),
                            lambda i: (layer, tile_in(i) // steps_per_group, 0, 0))
    return pl.pallas_call(
        functools.partial(_ffn_kernel, groups=groups, rows=rows, final=final),
        grid=(n_tiles + 1,),
        in_specs=[pl.BlockSpec((tile, D_MODEL), lambda i: (tile_in(i), 0)), mem_spec, mem_spec,
                  _resident((1, D_MODEL), layer),
                  _resident((D_MODEL, MEM_DIM), layer),
                  _resident((MEM_DIM, D_MODEL), layer),
                  _resident((1, D_MODEL), layer),
                  _resident((D_MODEL, D_FF), layer),
                  _resident((D_FF, D_MODEL), layer),
                  pl.BlockSpec((1, D_MODEL), lambda i: (0, 0))],
        out_specs=pl.BlockSpec((tile, D_MODEL), lambda i: (jnp.maximum(i - 1, 0), 0)),
        out_shape=jax.ShapeDtypeStruct(x2.shape, F32),
        scratch_shapes=[
            pltpu.VMEM((tile, D_MODEL), BF16),
            pltpu.VMEM((tile, MEM_DIM), BF16),
            pltpu.VMEM((tile, MEM_DIM), BF16),
            pltpu.VMEM((2, tile, D_MODEL), F32),
            pltpu.VMEM((2, tile, D_MODEL), BF16),
            pltpu.VMEM((tile, D_FF), BF16),
        ],
        compiler_params=_params(1),
        name=name,
    )(x2, mk, mv, w["g_cross"], w["w_cq"], w["w_co"], w["g_mlp"], w["w_up"], w["w_down"], w["g_final"])


def _memory_kv(mem, g_mem, w_ckv):
    batch = mem.shape[0]
    out = jax.ShapeDtypeStruct((DEPTH, batch, N_MEM, MEM_DIM), F32)
    return pl.pallas_call(
        _memkv_kernel,
        grid=(DEPTH,),
        in_specs=[pl.BlockSpec((batch, N_MEM, D_MODEL), lambda l: (0, 0, 0)),
                  pl.BlockSpec((None, 1, D_MODEL), lambda l: (l, 0, 0)),
                  pl.BlockSpec((None, D_MODEL, 2 * MEM_DIM), lambda l: (l, 0, 0))],
        out_specs=[pl.BlockSpec((None, batch, N_MEM, MEM_DIM), lambda l: (l, 0, 0, 0))] * 2,
        out_shape=[out, out],
        compiler_params=_params(1),
        name="memory_kv",
    )(mem, g_mem, w_ckv)


def kernel(x_prompt, x_sample, mem_prompt, cache_attn_k, cache_attn_v, state_conv, cache_mem_k, cache_mem_v,
           norm_mix_g, w_in, conv_w, attn_sink, w_attn_out, w_conv_out, w_mix_out, norm_cross_g, norm_mem_g,
           w_cq, w_ckv, w_co, norm_mlp_g, w_up, w_down, norm_final_g):
    batch, seq, _ = x_prompt.shape
    dec_batch, dec_seq, _ = x_sample.shape
    assert seq % PROMPT_TILE == 0 and PROMPT_TILE % LANES == 0 and PROMPT_TILE >= WINDOW
    assert seq % FFN_TILE == 0
    assert dec_seq <= CHUNK and dec_seq % 16 == 0 and dec_batch % SAMPLE_SEQS == 0
    assert cache_attn_k.shape[2] == WINDOW

    w = {
        "sink": attn_sink,
        "g_mix": norm_mix_g[:, None, :],
        "w_in": w_in.astype(BF16),
        "conv_w": conv_w,
        "w_attn_out": w_attn_out.astype(BF16),
        "w_conv_out": w_conv_out.astype(BF16),
        "w_mix_out": w_mix_out.astype(BF16),
        "g_cross": norm_cross_g[:, None, :],
        "w_cq": w_cq.astype(BF16),
        "w_co": w_co.astype(BF16),
        "g_mlp": norm_mlp_g[:, None, :],
        "w_up": w_up.astype(BF16),
        "w_down": w_down.astype(BF16),
        "g_final": norm_final_g[None, :],
    }
    slopes = jnp.exp2(-8.0 * jnp.arange(1, N_HEADS + 1, dtype=F32) / N_HEADS)

    mk_p, mv_p = _memory_kv(mem_prompt, norm_mem_g[:, None, :], w_ckv.astype(BF16))

    kc = cache_attn_k.reshape(DEPTH, dec_batch, WINDOW, KV_DIM)
    vc = cache_attn_v.reshape(DEPTH, dec_batch, WINDOW, KV_DIM)
    mk_s = cache_mem_k.reshape(DEPTH, dec_batch, N_MEM, MEM_DIM)
    mv_s = cache_mem_v.reshape(DEPTH, dec_batch, N_MEM, MEM_DIM)

    xp = x_prompt
    xs = x_sample.reshape(dec_batch * dec_seq, D_MODEL)
    kp_l, vp_l, cp_l, ks_l, vs_l, cs_l = [], [], [], [], [], []
    for layer in range(DEPTH):
        final = layer == DEPTH - 1
        xp, kp, vp, cp = _mixer_prompt(layer, xp, slopes, w)
        xp = _ffn(layer, xp.reshape(batch * seq, D_MODEL), mk_p, mv_p, w, 1, FFN_TILE, final,
                  f"ffn_prompt_l{layer}").reshape(batch, seq, D_MODEL)
        xs, kn, vn, cs = _mixer_sample(layer, xs, slopes, kc, vc, state_conv, w, dec_seq)
        xs = _ffn(layer, xs, mk_s, mv_s, w, SAMPLE_SEQS, dec_seq, final, f"ffn_sample_l{layer}")
        kp_l.append(kp)
        vp_l.append(vp)
        cp_l.append(cp)
        ks_l.append(kn)
        vs_l.append(vn)
        cs_l.append(cs)

    kv_p = (DEPTH, batch, WINDOW, N_KV_HEADS, HEAD_DIM)
    kv_s = (DEPTH, dec_batch, dec_seq, N_KV_HEADS, HEAD_DIM)
    mem_shape = (DEPTH, batch, N_MEM, MEM_HEADS, MEM_DIM // MEM_HEADS)
    return (xp, xs.reshape(dec_batch, dec_seq, D_MODEL),
            jnp.stack(kp_l).reshape(kv_p), jnp.stack(vp_l).reshape(kv_p), jnp.stack(cp_l),
            mk_p.reshape(mem_shape), mv_p.reshape(mem_shape),
            jnp.stack(ks_l).reshape(kv_s), jnp.stack(vs_l).reshape(kv_s), jnp.stack(cs_l))
```
